```python
import jax, jax.numpy as jnp
from jax import lax
import numpy as np

D_MODEL = 2048
BATCH = 8
SEQ = 2048
DEPTH = 2
DEC_BATCH = 128
DEC_SEQ = 8
PAST_LEN = 2048
PAGE_SIZE = 128

SB_HEADS = 8
SB_HEAD_DIM = D_MODEL // 16
SB_WIDTH = SB_HEADS * SB_HEAD_DIM
QUERY_BLOCK = 128
RET_HEADS = 8
RET_KEY_DIM = D_MODEL // 16
RET_VAL_DIM = D_MODEL // 16
RET_QK_WIDTH = RET_HEADS * RET_KEY_DIM
RET_V_WIDTH = RET_HEADS * RET_VAL_DIM
RET_CHUNK = 128
ROPE_BASE = 10000.0
D_IN = 3 * SB_WIDTH + 2 * RET_QK_WIDTH + 2 * RET_V_WIDTH + 2 * D_MODEL
D_FF_DENSE = D_MODEL * 11 // 4
N_EXPERTS = 8
TOP_K = 2
D_FF_EXPERT = D_MODEL * 7 // 2
N_DENSE = (DEPTH + 1) // 2
N_MOE = DEPTH // 2
EPS = 1e-6

kernel_name = 'stickbreak_retention_hybrid_step'


def _rms(x):
    xf = x.astype(jnp.float32)
    return xf * lax.rsqrt(jnp.mean(xf * xf, axis=-1, keepdims=True) + EPS)


def rms_norm(x, g):
    return (_rms(x) * g.astype(jnp.float32)).astype(x.dtype)


def rope(x, pos):
    half = x.shape[-1] // 2
    inv_freq = ROPE_BASE ** (-jnp.arange(half, dtype=jnp.float32) / half)
    ang = pos.astype(jnp.float32)[:, None] * inv_freq[None, :]
    cos = jnp.cos(ang)[:, None, :]
    sin = jnp.sin(ang)[:, None, :]
    xf = x.astype(jnp.float32)
    x1, x2 = xf[..., :half], xf[..., half:]
    return jnp.concatenate([x1 * cos - x2 * sin, x1 * sin + x2 * cos], axis=-1).astype(x.dtype)


def in_projection(xn, w_in, qnorm_g, knorm_g, pos):
    B, L, _ = xn.shape
    sizes = [SB_WIDTH, SB_WIDTH, SB_WIDTH, RET_QK_WIDTH, RET_QK_WIDTH, RET_V_WIDTH, RET_V_WIDTH, D_MODEL, D_MODEL]
    offs = np.cumsum(sizes)[:-1].tolist()
    sq, sk, sv, rq, rk, rv, rg, ga, gb = jnp.split(xn @ w_in, offs, axis=-1)
    sq = rms_norm(sq.reshape(B, L, SB_HEADS, SB_HEAD_DIM), qnorm_g)
    sk = rms_norm(sk.reshape(B, L, SB_HEADS, SB_HEAD_DIM), knorm_g)
    sv = sv.reshape(B, L, SB_HEADS, SB_HEAD_DIM)
    rq = rope(rq.reshape(B, L, RET_HEADS, RET_KEY_DIM), pos)
    rk = rope(rk.reshape(B, L, RET_HEADS, RET_KEY_DIM), pos) * (RET_KEY_DIM ** -0.5)
    rv = rv.reshape(B, L, RET_HEADS, RET_VAL_DIM)
    return sq, sk, sv, rq, rk, rv, rg, ga, gb


def sb_attend(q, k, v, q_pos, bias):
    z = jnp.einsum('bqhd,bkhd->bhqk', q, k).astype(jnp.float32) * (SB_HEAD_DIM ** -0.5)
    z = z + bias.astype(jnp.float32)[None, :, None, None]
    k_pos = jnp.arange(k.shape[1])
    valid = k_pos[None, :] < q_pos[:, None]
    log_beta = jax.nn.log_sigmoid(z)
    log_1m = jnp.where(valid, log_beta - z, 0.0)
    later = lax.cumsum(log_1m, axis=3, reverse=True) - log_1m
    a = jnp.where(valid, jnp.exp(log_beta + later), 0.0)
    return jnp.einsum('bhqk,bkhd->bqhd', a, v.astype(jnp.float32)).astype(q.dtype)


def sb_prompt(q, k, v, bias):
    B, S, H, D = q.shape
    nb = S // QUERY_BLOCK
    qb = q.reshape(B, nb, QUERY_BLOCK, H, D).transpose(1, 0, 2, 3, 4)
    pos = jnp.arange(S).reshape(nb, QUERY_BLOCK)
    out = lax.map(lambda a: sb_attend(a[0], k, v, a[1], bias), (qb, pos))
    return out.transpose(1, 0, 2, 3, 4).reshape(B, S, H, D)


def ret_chunk(S, q, k, v, log_gamma):
    q = q.astype(jnp.float32)
    k = k.astype(jnp.float32)
    v = v.astype(jnp.float32)
    S = S.astype(jnp.float32)
    L = q.shape[1]
    idx = jnp.arange(L, dtype=jnp.float32)
    diff = idx[:, None] - idx[None, :]
    causal = diff >= 0
    decay = jnp.where(causal[None], jnp.exp(jnp.where(causal, diff, 0.0)[None] * log_gamma[:, None, None]), 0.0)
    scores = jnp.einsum('bihd,bjhd->bhij', q, k) * decay[None]
    q_decay = jnp.exp((idx + 1.0)[:, None] * log_gamma[None, :])
    o = jnp.einsum('bhij,bjhe->bihe', scores, v) + jnp.einsum('bihd,bhde->bihe', q, S) * q_decay[None, :, :, None]
    k_decay = jnp.exp((L - 1.0 - idx)[:, None] * log_gamma[None, :])
    S_new = jnp.exp(L * log_gamma)[None, :, None, None] * S + jnp.einsum('bjhd,bjhe->bhde', k * k_decay[None, :, :, None], v)
    return o, S_new


def retention_prompt(q, k, v, log_gamma):
    B, S, H, dk = q.shape
    dv = v.shape[-1]
    nc = S // RET_CHUNK
    to_chunks = lambda t: t.reshape(B, nc, RET_CHUNK, H, t.shape[-1]).transpose(1, 0, 2, 3, 4)
    S0 = jnp.zeros((B, H, dk, dv), jnp.float32)

    def step(state, xs):
        o, state = ret_chunk(state, xs[0], xs[1], xs[2], log_gamma)
        return state, o

    S_fin, o = lax.scan(step, S0, (to_chunks(q), to_chunks(k), to_chunks(v)))
    return o.transpose(1, 0, 2, 3, 4).reshape(B, S, H, dv), S_fin


def branch_merge(o_sb, o_ret, rg, ga, gb, w_pa, w_pb, w_o):
    B, L = o_sb.shape[:2]
    dt = ga.dtype
    ret = (_rms(o_ret) * jax.nn.silu(rg.reshape(B, L, RET_HEADS, RET_VAL_DIM).astype(jnp.float32)))
    ret = ret.reshape(B, L, RET_V_WIDTH).astype(dt)
    ya = o_sb.reshape(B, L, SB_WIDTH) @ w_pa
    yb = ret @ w_pb
    return (jax.nn.sigmoid(ga) * ya + jax.nn.sigmoid(gb) * yb) @ w_o


def swiglu(x, wg, wu, wd):
    return (jax.nn.silu(x @ wg) * (x @ wu)) @ wd


def moe(x, w_router, wg, wu, wd):
    logits = (x @ w_router).astype(jnp.float32)
    top_vals, top_idx = lax.top_k(logits, TOP_K)
    gates = jax.nn.softmax(top_vals, axis=-1)
    comb = jnp.sum(jax.nn.one_hot(top_idx, N_EXPERTS, dtype=jnp.float32) * gates[..., None], axis=-2)
    out = jnp.zeros(x.shape, jnp.float32)
    for e in range(N_EXPERTS):
        out = out + comb[..., e:e + 1] * swiglu(x, wg[e], wu[e], wd[e]).astype(jnp.float32)
    return out.astype(x.dtype)


def setup_inputs(seed: int = 0) -> dict:
    key = jax.random.key(seed)
    ks = jax.random.split(key, 24)
    n_pages = PAST_LEN // PAGE_SIZE
    n_used = DEC_BATCH * n_pages
    n_phys = n_used + n_used // 4
    f32 = jnp.float32
    nrm = lambda k, shape, s=1.0: jax.random.normal(k, shape, f32) * s
    gain = lambda k, shape: 1.0 + 0.02 * jax.random.normal(k, shape, f32)
    page_table = jax.random.permutation(ks[5], n_phys)[:n_used].reshape(DEC_BATCH, n_pages).astype(jnp.int32)
    sb_bias = -(6.0 + 0.5 * jnp.arange(SB_HEADS, dtype=f32))[None, :] + 0.1 * nrm(ks[21], (DEPTH, SB_HEADS))
    return {
        'x_prompt': nrm(ks[0], (BATCH, SEQ, D_MODEL)),
        'x_sample': nrm(ks[1], (DEC_BATCH, DEC_SEQ, D_MODEL)),
        'cache_sb_k': nrm(ks[2], (DEPTH, n_phys, PAGE_SIZE, SB_HEADS, SB_HEAD_DIM)),
        'cache_sb_v': nrm(ks[3], (DEPTH, n_phys, PAGE_SIZE, SB_HEADS, SB_HEAD_DIM)),
        'state_ret': nrm(ks[4], (DEPTH, DEC_BATCH, RET_HEADS, RET_KEY_DIM, RET_VAL_DIM), 0.5),
        'page_table': page_table,
        'norm_attn': gain(ks[6], (DEPTH, D_MODEL)),
        'w_in': nrm(ks[7], (DEPTH, D_MODEL, D_IN), D_MODEL ** -0.5),
        'qnorm_g': gain(ks[8], (DEPTH, SB_HEAD_DIM)),
        'knorm_g': gain(ks[9], (DEPTH, SB_HEAD_DIM)),
        'sb_bias': sb_bias,
        'w_pa': nrm(ks[10], (DEPTH, SB_WIDTH, D_MODEL), SB_WIDTH ** -0.5),
        'w_pb': nrm(ks[11], (DEPTH, RET_V_WIDTH, D_MODEL), RET_V_WIDTH ** -0.5),
        'w_o': nrm(ks[12], (DEPTH, D_MODEL, D_MODEL), D_MODEL ** -0.5),
        'norm_ffn': gain(ks[13], (DEPTH, D_MODEL)),
        'w_ff_gate': nrm(ks[14], (N_DENSE, D_MODEL, D_FF_DENSE), D_MODEL ** -0.5),
        'w_ff_up': nrm(ks[15], (N_DENSE, D_MODEL, D_FF_DENSE), D_MODEL ** -0.5),
        'w_ff_down': nrm(ks[16], (N_DENSE, D_FF_DENSE, D_MODEL), D_FF_DENSE ** -0.5),
        'w_router': nrm(ks[17], (N_MOE, D_MODEL, N_EXPERTS), D_MODEL ** -0.5),
        'w_exp_gate': nrm(ks[18], (N_MOE, N_EXPERTS, D_MODEL, D_FF_EXPERT), D_MODEL ** -0.5),
        'w_exp_up': nrm(ks[19], (N_MOE, N_EXPERTS, D_MODEL, D_FF_EXPERT), D_MODEL ** -0.5),
        'w_exp_down': nrm(ks[20], (N_MOE, N_EXPERTS, D_FF_EXPERT, D_MODEL), D_FF_EXPERT ** -0.5),
    }


def reference(x_prompt, x_sample, cache_sb_k, cache_sb_v, state_ret, page_table, norm_attn, w_in, qnorm_g, knorm_g,
              sb_bias, w_pa, w_pb, w_o, norm_ffn, w_ff_gate, w_ff_up, w_ff_down, w_router, w_exp_gate, w_exp_up,
              w_exp_down):
    n_pages = page_table.shape[1]
    page = cache_sb_k.shape[2]
    past_len = n_pages * page
    DB, Ls = x_sample.shape[0], x_sample.shape[1]
    pos_p = jnp.arange(x_prompt.shape[1])
    pos_s = past_len + jnp.arange(Ls)
    log_gamma = jnp.log1p(-jnp.exp2(-5.0 - jnp.arange(RET_HEADS, dtype=jnp.float32)))

    xp, xs = x_prompt, x_sample
    kp_rows, vp_rows, sp_states, ks_rows, vs_rows, ss_states = [], [], [], [], [], []
    for l in range(DEPTH):
        hp = rms_norm(xp, norm_attn[l])
        sq, sk, sv, rq, rk, rv, rg, ga, gb = in_projection(hp, w_in[l], qnorm_g[l], knorm_g[l], pos_p)
        o_sb = sb_prompt(sq, sk, sv, sb_bias[l])
        o_ret, S_p = retention_prompt(rq, rk, rv, log_gamma)
        xp = xp + branch_merge(o_sb, o_ret, rg, ga, gb, w_pa[l], w_pb[l], w_o[l])
        kp_rows.append(sk)
        vp_rows.append(sv)
        sp_states.append(S_p.astype(state_ret.dtype))

        hs = rms_norm(xs, norm_attn[l])
        tq, tk, tv, uq, uk, uv, ug, ha, hb = in_projection(hs, w_in[l], qnorm_g[l], knorm_g[l], pos_s)
        k_past = cache_sb_k[l][page_table].reshape(DB, past_len, SB_HEADS, SB_HEAD_DIM).astype(tk.dtype)
        v_past = cache_sb_v[l][page_table].reshape(DB, past_len, SB_HEADS, SB_HEAD_DIM).astype(tv.dtype)
        o_sb_s = sb_attend(tq, jnp.concatenate([k_past, tk], axis=1), jnp.concatenate([v_past, tv], axis=1),
                           pos_s, sb_bias[l])
        o_ret_s, S_s = ret_chunk(state_ret[l], uq, uk, uv, log_gamma)
        xs = xs + branch_merge(o_sb_s, o_ret_s, ug, ha, hb, w_pa[l], w_pb[l], w_o[l])
        ks_rows.append(tk)
        vs_rows.append(tv)
        ss_states.append(S_s.astype(state_ret.dtype))

        i = l // 2
        if l % 2 == 0:
            xp = xp + swiglu(rms_norm(xp, norm_ffn[l]), w_ff_gate[i], w_ff_up[i], w_ff_down[i])
            xs = xs + swiglu(rms_norm(xs, norm_ffn[l]), w_ff_gate[i], w_ff_up[i], w_ff_down[i])
        else:
            xp = xp + moe(rms_norm(xp, norm_ffn[l]), w_router[i], w_exp_gate[i], w_exp_up[i], w_exp_down[i])
            xs = xs + moe(rms_norm(xs, norm_ffn[l]), w_router[i], w_exp_gate[i], w_exp_up[i], w_exp_down[i])

    return (xp, xs, jnp.stack(kp_rows), jnp.stack(vp_rows), jnp.stack(sp_states),
            jnp.stack(ks_rows), jnp.stack(vs_rows), jnp.stack(ss_states))
```

```python
import functools

import jax
import jax.numpy as jnp
import numpy as np
from jax import lax
from jax.experimental import pallas as pl
from jax.experimental.pallas import tpu as pltpu

F32 = jnp.float32
BF16 = jnp.bfloat16

HEAD_DIM = 128
N_HEADS = 8
WIDTH = N_HEADS * HEAD_DIM
PAGE = 128
TOP_K = 2
EPS = 1e-6
ROPE_BASE = 10000.0
NEG_INF = float("-inf")

SEC_SQ, SEC_SK, SEC_SV, SEC_RQ, SEC_RK, SEC_RV, SEC_RG = (s * N_HEADS for s in range(7))
COL_GA = 7 * WIDTH
D_IN_SECTIONS = 7

VMEM_LIMIT = 56 * 1024 * 1024


def _cparams(*semantics):
    return pltpu.CompilerParams(dimension_semantics=semantics, vmem_limit_bytes=VMEM_LIMIT)


def _dot(a, b):
    return jnp.dot(a, b, preferred_element_type=F32)


def _dot_nt(a, b):
    return lax.dot_general(a, b, (((1,), (1,)), ((), ())), preferred_element_type=F32)


def _sigmoid(x):
    return 1.0 / (1.0 + jnp.exp(-x))


def _rms_rows(x):
    return x * lax.rsqrt(jnp.mean(x * x, axis=-1, keepdims=True) + EPS)


def _inproj_kernel(x_ref, g_ref, w_ref, rope_ref, qg_ref, kg_ref, o_ref, h_ref, *, tn):
    j = pl.program_id(1)

    @pl.when(j == 0)
    def _():
        h_ref[...] = (_rms_rows(x_ref[...]) * g_ref[...]).astype(BF16)

    acc = _dot(h_ref[...], w_ref[...].astype(BF16))
    sec = j // (WIDTH // tn)
    heads = tn // HEAD_DIM

    @pl.when(sec <= 1)
    def _():
        gain = jnp.where(sec == 0, qg_ref[...], kg_ref[...])
        for hh in range(heads):
            sl = slice(hh * HEAD_DIM, (hh + 1) * HEAD_DIM)
            o_ref[:, sl] = _rms_rows(acc[:, sl]) * gain

    @pl.when(jnp.logical_or(sec == 3, sec == 4))
    def _():
        cos = rope_ref[:, :HEAD_DIM]
        sin = rope_ref[:, HEAD_DIM:]
        scale = jnp.where(sec == 4, HEAD_DIM ** -0.5, 1.0).astype(F32)
        for hh in range(heads):
            sl = slice(hh * HEAD_DIM, (hh + 1) * HEAD_DIM)
            blk = acc[:, sl]
            o_ref[:, sl] = (blk * cos + pltpu.roll(blk, HEAD_DIM // 2, 1) * sin) * scale

    @pl.when(jnp.logical_and(sec != 0, jnp.logical_and(sec != 1, jnp.logical_and(sec != 3, sec != 4))))
    def _():
        o_ref[...] = acc


def _rope_table(seq, dec_seq, past_len, tm):
    half = HEAD_DIM // 2
    inv_freq = ROPE_BASE ** (-jnp.arange(half, dtype=F32) / half)
    pos = jnp.concatenate([jnp.arange(seq), past_len + (jnp.arange(tm) % dec_seq)]).astype(F32)
    ang = pos[:, None] * inv_freq[None, :]
    cos, sin = jnp.cos(ang), jnp.sin(ang)
    return jnp.concatenate([cos, cos, -sin, sin], axis=-1)


def _in_projection(x, g, w, rope_tab, qg, kg, *, n_prompt, seq, tm, tn):
    t, d = x.shape
    d_in = w.shape[1]
    assert t % tm == 0 and d_in % tn == 0 and WIDTH % tn == 0 and seq % tm == 0 and n_prompt % tm == 0
    n_prompt_tiles = n_prompt // tm
    per_seq = seq // tm

    def rope_map(i, j):
        return (jnp.where(i < n_prompt_tiles, i % per_seq, per_seq), 0)

    return pl.pallas_call(
        functools.partial(_inproj_kernel, tn=tn),
        grid=(t // tm, d_in // tn),
        in_specs=[
            pl.BlockSpec((tm, d), lambda i, j: (i, 0)),
            pl.BlockSpec((1, d), lambda i, j: (0, 0)),
            pl.BlockSpec((d, tn), lambda i, j: (0, j)),
            pl.BlockSpec((tm, 2 * HEAD_DIM), rope_map),
            pl.BlockSpec((1, HEAD_DIM), lambda i, j: (0, 0)),
            pl.BlockSpec((1, HEAD_DIM), lambda i, j: (0, 0)),
        ],
        out_specs=pl.BlockSpec((tm, tn), lambda i, j: (i, j)),
        out_shape=jax.ShapeDtypeStruct((t, d_in), F32),
        scratch_shapes=[pltpu.VMEM((tm, d), BF16)],
        compiler_params=_cparams("parallel", "arbitrary"),
        name="in_projection",
    )(x, g.reshape(1, d), w, rope_tab, qg.reshape(1, HEAD_DIM), kg.reshape(1, HEAD_DIM))


def _sb_block(z, valid, tri, carry):
    soft = jnp.log(1.0 + jnp.exp(-jnp.abs(z)))
    log_beta = jnp.minimum(z, 0.0) - soft
    log_1m = -jnp.maximum(z, 0.0) - soft
    if valid is not None:
        log_1m = jnp.where(valid, log_1m, 0.0)
    hi = log_1m.astype(BF16)
    lo = (log_1m - hi.astype(F32)).astype(BF16)
    later = _dot(hi, tri) + _dot(lo, tri)
    a = jnp.exp(log_beta + later + carry)
    if valid is not None:
        a = jnp.where(valid, a, 0.0)
    return a, carry + jnp.sum(log_1m, axis=-1, keepdims=True)


def _tri(n):
    idx = jnp.arange(n)
    return (idx[:, None] > idx[None, :]).astype(BF16)


def _sb_prompt_kernel(bias_ref, q_ref, k_ref, v_ref, tri_ref, o_ref, kb_ref, vb_ref, *, tq):
    h = pl.program_id(1)
    qi = pl.program_id(2)

    @pl.when(qi == 0)
    def _():
        kb_ref[...] = k_ref[...].astype(BF16)
        vb_ref[...] = v_ref[...].astype(BF16)

    q = q_ref[...].astype(BF16)
    bias = bias_ref[h]
    tri = tri_ref[...]
    scale = HEAD_DIM ** -0.5
    row = lax.broadcasted_iota(jnp.int32, (tq, tq), 0)
    col = lax.broadcasted_iota(jnp.int32, (tq, tq), 1)

    def block(kb, carry, acc, valid):
        ks = pl.multiple_of(kb * tq, tq)
        z = _dot_nt(q, kb_ref[pl.ds(ks, tq), :]) * scale + bias
        a, carry = _sb_block(z, valid, tri, carry)
        return carry, acc + _dot(a.astype(BF16), vb_ref[pl.ds(ks, tq), :])

    carry0 = jnp.zeros((tq, 1), F32)
    acc0 = jnp.zeros((tq, HEAD_DIM), F32)
    carry, acc = block(qi, carry0, acc0, col < row)

    def body(it, ca):
        return block(qi - 1 - it, ca[0], ca[1], None)

    carry, acc = lax.fori_loop(0, qi, body, (carry, acc))
    o_ref[...] = acc.astype(o_ref.dtype)


def _sb_prompt(p, bias, *, batch, seq, tq):
    nq = seq // tq
    return pl.pallas_call(
        functools.partial(_sb_prompt_kernel, tq=tq),
        grid_spec=pltpu.PrefetchScalarGridSpec(
            num_scalar_prefetch=0,
            grid=(batch, N_HEADS, nq),
            in_specs=[
                pl.BlockSpec(memory_space=pltpu.SMEM),
                pl.BlockSpec((tq, HEAD_DIM), lambda b, h, qi: (b * nq + qi, SEC_SQ + h)),
                pl.BlockSpec((seq, HEAD_DIM), lambda b, h, qi: (b, SEC_SK + h)),
                pl.BlockSpec((seq, HEAD_DIM), lambda b, h, qi: (b, SEC_SV + h)),
                pl.BlockSpec((tq, tq), lambda b, h, qi: (0, 0)),
            ],
            out_specs=pl.BlockSpec((tq, HEAD_DIM), lambda b, h, qi: (b * nq + qi, h)),
            scratch_shapes=[pltpu.VMEM((seq, HEAD_DIM), BF16), pltpu.VMEM((seq, HEAD_DIM), BF16)],
        ),
        out_shape=jax.ShapeDtypeStruct((batch * seq, WIDTH), BF16),
        compiler_params=_cparams("parallel", "parallel", "arbitrary"),
        name="sb_prompt",
    )(bias, p, p, p, _tri(tq))


def _sb_decode_kernel(pt_ref, q_ref, kn_ref, vn_ref, *rest, n_group, dec_seq):
    k_refs = rest[:n_group]
    v_refs = rest[n_group:2 * n_group]
    bias_ref, tri_ref, o_ref, kpad_ref, vpad_ref, carry_ref, acc_ref = rest[2 * n_group:]
    g = pl.program_id(1)
    rows = N_HEADS * dec_seq
    scale = HEAD_DIM ** -0.5
    tri = tri_ref[...]
    bias = bias_ref[...]
    qs = [q_ref[:, h * HEAD_DIM:(h + 1) * HEAD_DIM].astype(BF16) for h in range(N_HEADS)]

    def process(k_ref, v_ref, valid):
        z = jnp.concatenate([_dot_nt(qs[h], k_ref[:, h, :].astype(BF16)) for h in range(N_HEADS)], axis=0)
        a, carry = _sb_block(z * scale + bias, valid, tri, carry_ref[...])
        carry_ref[...] = carry
        for h in range(N_HEADS):
            ah = a[h * dec_seq:(h + 1) * dec_seq, :].astype(BF16)
            sl = slice(h * HEAD_DIM, (h + 1) * HEAD_DIM)
            acc_ref[:, sl] = acc_ref[:, sl] + _dot(ah, v_ref[:, h, :].astype(BF16))

    @pl.when(g == 0)
    def _():
        carry_ref[...] = jnp.zeros_like(carry_ref)
        acc_ref[...] = jnp.zeros_like(acc_ref)
        kpad_ref[...] = jnp.zeros_like(kpad_ref)
        vpad_ref[...] = jnp.zeros_like(vpad_ref)
        kpad_ref[:dec_seq] = kn_ref[...]
        vpad_ref[:dec_seq] = vn_ref[...]
        t_idx = lax.broadcasted_iota(jnp.int32, (rows, PAGE), 0) % dec_seq
        s_idx = lax.broadcasted_iota(jnp.int32, (rows, PAGE), 1)
        process(kpad_ref, vpad_ref, s_idx < t_idx)

    for r in range(n_group - 1, -1, -1):
        process(k_refs[r], v_refs[r], None)

    @pl.when(g == pl.num_programs(1) - 1)
    def _():
        o_ref[...] = acc_ref[...]


def _sb_decode(p, k_new, v_new, cache_k, cache_v, page_table, bias, *, layer, n_prompt, n_group):
    dec_batch, dec_seq = k_new.shape[:2]
    n_pages = page_table.shape[1]
    assert n_pages % n_group == 0 and dec_seq == 8
    n_steps = n_pages // n_group
    rows = N_HEADS * dec_seq
    q_row0 = n_prompt // dec_seq

    def page_map(r):
        def index(b, g, pt):
            return (layer, pt[b * n_pages + n_pages - (g + 1) * n_group + r], 0, 0, 0)
        return index

    page_specs = [pl.BlockSpec((None, None, PAGE, N_HEADS, HEAD_DIM), page_map(r)) for r in range(n_group)]
    new_spec = pl.BlockSpec((None, dec_seq, N_HEADS, HEAD_DIM), lambda b, g, pt: (b, 0, 0, 0))
    bias_rows = jnp.broadcast_to(jnp.repeat(bias.astype(F32), dec_seq)[:, None], (rows, PAGE))
    return pl.pallas_call(
        functools.partial(_sb_decode_kernel, n_group=n_group, dec_seq=dec_seq),
        grid_spec=pltpu.PrefetchScalarGridSpec(
            num_scalar_prefetch=1,
            grid=(dec_batch, n_steps),
            in_specs=[pl.BlockSpec((dec_seq, WIDTH), lambda b, g, pt: (q_row0 + b, SEC_SQ // N_HEADS)),
                      new_spec, new_spec] + page_specs + page_specs + [
                pl.BlockSpec((rows, PAGE), lambda b, g, pt: (0, 0)),
                pl.BlockSpec((PAGE, PAGE), lambda b, g, pt: (0, 0)),
            ],
            out_specs=pl.BlockSpec((dec_seq, WIDTH), lambda b, g, pt: (b, 0)),
            scratch_shapes=[
                pltpu.VMEM((PAGE, N_HEADS, HEAD_DIM), F32),
                pltpu.VMEM((PAGE, N_HEADS, HEAD_DIM), F32),
                pltpu.VMEM((rows, 1), F32),
                pltpu.VMEM((dec_seq, WIDTH), F32),
            ],
        ),
        out_shape=jax.ShapeDtypeStruct((dec_batch * dec_seq, WIDTH), F32),
        compiler_params=_cparams("parallel", "arbitrary"),
        name="sb_decode",
    )(page_table.reshape(-1), p, k_new, v_new, *([cache_k] * n_group), *([cache_v] * n_group),
      bias_rows, _tri(PAGE))


def _ret_consts(log_gamma, chunk):
    idx = jnp.arange(chunk, dtype=F32)
    diff = idx[:, None] - idx[None, :]
    causal = diff >= 0
    decay = jnp.where(causal[None], jnp.exp(jnp.where(causal, diff, 0.0)[None] * log_gamma[:, None, None]), 0.0)
    q_decay = jnp.exp((idx + 1.0)[None, :] * log_gamma[:, None])
    k_decay = jnp.exp((chunk - 1.0 - idx)[None, :] * log_gamma[:, None])
    lanes = (N_HEADS, chunk, HEAD_DIM)
    return (decay, jnp.broadcast_to(q_decay[:, :, None], lanes), jnp.broadcast_to(k_decay[:, :, None], lanes),
            jnp.exp(chunk * log_gamma))


def _ret_chunk(q, k, v, rg, state, decay, qd, kd, state_decay):
    qb, kb, vb = q.astype(BF16), k.astype(BF16), v.astype(BF16)
    scores = _dot_nt(qb, kb) * decay
    o = _dot(scores.astype(BF16), vb) + _dot(qb, state.astype(BF16)) * qd
    new_state = state_decay * state + lax.dot_general((k * kd).astype(BF16), vb, (((0,), (0,)), ((), ())),
                                                      preferred_element_type=F32)
    out = _rms_rows(o) * (rg * _sigmoid(rg))
    return out, new_state


def _ret_prompt_kernel(sd_ref, q_ref, k_ref, v_ref, rg_ref, decay_ref, qd_ref, kd_ref, o_ref, s_ref, *, chunk, n_chunks):
    h = pl.program_id(1)
    s_ref[...] = jnp.zeros_like(s_ref)

    def body(c, _):
        sl = pl.ds(pl.multiple_of(c * chunk, chunk), chunk)
        out, new_state = _ret_chunk(q_ref[sl, :], k_ref[sl, :], v_ref[sl, :], rg_ref[sl, :], s_ref[...],
                                    decay_ref[...], qd_ref[...], kd_ref[...], sd_ref[h])
        s_ref[...] = new_state
        o_ref[sl, :] = out.astype(o_ref.dtype)
        return 0

    lax.fori_loop(0, n_chunks, body, 0)


def _ret_prompt(p, log_gamma, *, batch, seq, chunk):
    decay, qd, kd, sd = _ret_consts(log_gamma, chunk)
    col = lambda sec: pl.BlockSpec((seq, HEAD_DIM), lambda b, h: (b, sec + h))
    per_head = lambda n: pl.BlockSpec((None, chunk, n), lambda b, h: (h, 0, 0))
    return pl.pallas_call(
        functools.partial(_ret_prompt_kernel, chunk=chunk, n_chunks=seq // chunk),
        grid=(batch, N_HEADS),
        in_specs=[pl.BlockSpec(memory_space=pltpu.SMEM), col(SEC_RQ), col(SEC_RK), col(SEC_RV), col(SEC_RG),
                  per_head(chunk), per_head(HEAD_DIM), per_head(HEAD_DIM)],
        out_specs=[pl.BlockSpec((seq, HEAD_DIM), lambda b, h: (b, h)),
                   pl.BlockSpec((None, None, HEAD_DIM, HEAD_DIM), lambda b, h: (b, h, 0, 0))],
        out_shape=[jax.ShapeDtypeStruct((batch * seq, WIDTH), BF16),
                   jax.ShapeDtypeStruct((batch, N_HEADS, HEAD_DIM, HEAD_DIM), F32)],
        compiler_params=_cparams("parallel", "parallel"),
        name="retention_prompt",
    )(sd, p, p, p, p, decay, qd, kd)


def _ret_sample_kernel(sd_ref, q_ref, k_ref, v_ref, rg_ref, s_in_ref, decay_ref, qd_ref, kd_ref, o_ref, s_out_ref):
    for h in range(N_HEADS):
        sl = slice(h * HEAD_DIM, (h + 1) * HEAD_DIM)
        out, new_state = _ret_chunk(q_ref[:, sl], k_ref[:, sl], v_ref[:, sl], rg_ref[:, sl], s_in_ref[h],
                                    decay_ref[h], qd_ref[h], kd_ref[h], sd_ref[h])
        o_ref[:, sl] = out
        s_out_ref[h] = new_state


def _ret_sample(p, state, log_gamma, *, n_prompt, dec_seq):
    dec_batch = state.shape[0]
    decay, qd, kd, sd = _ret_consts(log_gamma, dec_seq)
    row0 = n_prompt // dec_seq
    col = lambda sec: pl.BlockSpec((dec_seq, WIDTH), lambda b: (row0 + b, sec // N_HEADS))
    full = lambda a: pl.BlockSpec(a.shape, lambda b: (0,) * a.ndim)
    state_spec = pl.BlockSpec((None, N_HEADS, HEAD_DIM, HEAD_DIM), lambda b: (b, 0, 0, 0))
    return pl.pallas_call(
        _ret_sample_kernel,
        grid=(dec_batch,),
        in_specs=[pl.BlockSpec(memory_space=pltpu.SMEM), col(SEC_RQ), col(SEC_RK), col(SEC_RV), col(SEC_RG),
                  state_spec, full(decay), full(qd), full(kd)],
        out_specs=[pl.BlockSpec((dec_seq, WIDTH), lambda b: (b, 0)), state_spec],
        out_shape=[jax.ShapeDtypeStruct((dec_batch * dec_seq, WIDTH), F32),
                   jax.ShapeDtypeStruct(state.shape, F32)],
        compiler_params=_cparams("parallel"),
        name="retention_sample",
    )(sd, p, p, p, p, state, decay, qd, kd)


def _merge_kernel(a_ref, b_ref, wa_ref, wb_ref, ga_ref, gb_ref, o_ref):
    ya = _dot(a_ref[...], wa_ref[...].astype(BF16))
    yb = _dot(b_ref[...], wb_ref[...].astype(BF16))
    o_ref[...] = (_sigmoid(ga_ref[...]) * ya + _sigmoid(gb_ref[...]) * yb).astype(o_ref.dtype)


def _merge(o_sb, ret, w_pa, w_pb, p, *, tm, tn):
    t = o_sb.shape[0]
    d = w_pa.shape[1]
    ga0 = COL_GA // tn
    gb0 = (COL_GA + d) // tn
    return pl.pallas_call(
        _merge_kernel,
        grid=(t // tm, d // tn),
        in_specs=[
            pl.BlockSpec((tm, WIDTH), lambda i, j: (i, 0)),
            pl.BlockSpec((tm, WIDTH), lambda i, j: (i, 0)),
            pl.BlockSpec((WIDTH, tn), lambda i, j: (0, j)),
            pl.BlockSpec((WIDTH, tn), lambda i, j: (0, j)),
            pl.BlockSpec((tm, tn), lambda i, j: (i, ga0 + j)),
            pl.BlockSpec((tm, tn), lambda i, j: (i, gb0 + j)),
        ],
        out_specs=pl.BlockSpec((tm, tn), lambda i, j: (i, j)),
        out_shape=jax.ShapeDtypeStruct((t, d), BF16),
        compiler_params=_cparams("parallel", "arbitrary"),
        name="branch_merge",
    )(o_sb, ret, w_pa, w_pb, p, p)


def _out_proj_kernel(m_ref, w_ref, x_ref, o_ref):
    o_ref[...] = x_ref[...] + _dot(m_ref[...], w_ref[...].astype(BF16))


def _out_proj(m, w_o, x, *, tm, tn):
    t, d = x.shape
    return pl.pallas_call(
        _out_proj_kernel,
        grid=(t // tm, d // tn),
        in_specs=[
            pl.BlockSpec((tm, d), lambda i, j: (i, 0)),
            pl.BlockSpec((d, tn), lambda i, j: (0, j)),
            pl.BlockSpec((tm, tn), lambda i, j: (i, j)),
        ],
        out_specs=pl.BlockSpec((tm, tn), lambda i, j: (i, j)),
        out_shape=jax.ShapeDtypeStruct((t, d), F32),
        compiler_params=_cparams("parallel", "arbitrary"),
        name="out_projection",
    )(m, w_o, x)


def _ffn_body(x_ref, g_ref, wg_ref, wu_ref, wd_ref, o_ref, h_ref, *, residual):
    f = pl.program_id(1)

    @pl.when(f == 0)
    def _():
        x = x_ref[...]
        h_ref[...] = (_rms_rows(x) * g_ref[...]).astype(BF16)
        o_ref[...] = x if residual else jnp.zeros_like(x)

    h = h_ref[...]
    gate = _dot(h, wg_ref[...].astype(BF16))
    up = _dot(h, wu_ref[...].astype(BF16))
    act = (gate * _sigmoid(gate) * up).astype(BF16)
    o_ref[...] += _dot(act, wd_ref[...].astype(BF16))


def _dense_ffn_kernel(x_ref, g_ref, wg_ref, wu_ref, wd_ref, o_ref, h_ref):
    _ffn_body(x_ref, g_ref, wg_ref, wu_ref, wd_ref, o_ref, h_ref, residual=True)


def _dense_ffn(x, g, wg, wu, wd, *, tm, tf):
    t, d = x.shape
    d_ff = wg.shape[1]
    return pl.pallas_call(
        _dense_ffn_kernel,
        grid=(t // tm, d_ff // tf),
        in_specs=[
            pl.BlockSpec((tm, d), lambda i, f: (i, 0), pipeline_mode=pl.Buffered(1)),
            pl.BlockSpec((1, d), lambda i, f: (0, 0)),
            pl.BlockSpec((d, tf), lambda i, f: (0, f)),
            pl.BlockSpec((d, tf), lambda i, f: (0, f)),
            pl.BlockSpec((tf, d), lambda i, f: (f, 0)),
        ],
        out_specs=pl.BlockSpec((tm, d), lambda i, f: (i, 0)),
        out_shape=jax.ShapeDtypeStruct((t, d), F32),
        scratch_shapes=[pltpu.VMEM((tm, d), BF16)],
        compiler_params=_cparams("parallel", "arbitrary"),
        name="dense_ffn",
    )(x, g.reshape(1, d), wg, wu, wd)


def _expert_ffn_kernel(te_ref, tv_ref, x_ref, g_ref, wg_ref, wu_ref, wd_ref, o_ref, h_ref):
    i = pl.program_id(0)

    @pl.when(tv_ref[i] > 0)
    def _():
        _ffn_body(x_ref, g_ref, wg_ref, wu_ref, wd_ref, o_ref, h_ref, residual=False)

    @pl.when(jnp.logical_and(tv_ref[i] == 0, pl.program_id(1) == 0))
    def _():
        o_ref[...] = jnp.zeros_like(o_ref)


def _expert_ffn(xs, g, wg, wu, wd, tile_expert, tile_valid, *, tm, tf):
    r, d = xs.shape
    d_ff = wg.shape[2]
    nf = d_ff // tf

    def f_idx(i, f, tv):
        return jnp.where(tv[i] > 0, f, nf - 1)

    return pl.pallas_call(
        _expert_ffn_kernel,
        grid_spec=pltpu.PrefetchScalarGridSpec(
            num_scalar_prefetch=2,
            grid=(r // tm, nf),
            in_specs=[
                pl.BlockSpec((tm, d), lambda i, f, te, tv: (i, 0), pipeline_mode=pl.Buffered(1)),
                pl.BlockSpec((1, d), lambda i, f, te, tv: (0, 0)),
                pl.BlockSpec((None, d, tf), lambda i, f, te, tv: (te[i], 0, f_idx(i, f, tv))),
                pl.BlockSpec((None, d, tf), lambda i, f, te, tv: (te[i], 0, f_idx(i, f, tv))),
                pl.BlockSpec((None, tf, d), lambda i, f, te, tv: (te[i], f_idx(i, f, tv), 0)),
            ],
            out_specs=pl.BlockSpec((tm, d), lambda i, f, te, tv: (i, 0)),
            scratch_shapes=[pltpu.VMEM((tm, d), BF16)],
        ),
        out_shape=jax.ShapeDtypeStruct((r, d), F32),
        compiler_params=_cparams("parallel", "arbitrary"),
        name="expert_ffn",
    )(tile_expert, tile_valid, xs, g.reshape(1, d), wg, wu, wd)


def _router_kernel(x_ref, g_ref, w_ref, o_ref, *, n_experts):
    h = _rms_rows(x_ref[...]) * g_ref[...]
    w = w_ref[...]
    h_hi = h.astype(BF16)
    h_lo = (h - h_hi.astype(F32)).astype(BF16)
    w_hi = w.astype(BF16)
    w_lo = (w - w_hi.astype(F32)).astype(BF16)
    logits = _dot(h_hi, w_hi) + (_dot(h_hi, w_lo) + _dot(h_lo, w_hi))
    lane = lax.broadcasted_iota(jnp.int32, logits.shape, 1).astype(F32)
    lg = jnp.where(lane < n_experts, logits, NEG_INF)
    m1 = jnp.max(lg, axis=-1, keepdims=True)
    i1 = jnp.min(jnp.where(lg == m1, lane, float(HEAD_DIM)), axis=-1, keepdims=True)
    lg2 = jnp.where(lane == i1, NEG_INF, lg)
    m2 = jnp.max(lg2, axis=-1, keepdims=True)
    i2 = jnp.min(jnp.where(lg2 == m2, lane, float(HEAD_DIM)), axis=-1, keepdims=True)
    e = jnp.exp(m2 - m1)
    g1 = 1.0 / (1.0 + e)
    g2 = e / (1.0 + e)
    o_ref[...] = jnp.where(lane == 0, i1, jnp.where(lane == 1, i2, jnp.where(lane == 2, g1, jnp.where(lane == 3, g2, 0.0))))


def _router(x, g, w_router, *, tm):
    t, d = x.shape
    n_experts = w_router.shape[1]
    w_pad = jnp.pad(w_router, ((0, 0), (0, HEAD_DIM - n_experts)))
    return pl.pallas_call(
        functools.partial(_router_kernel, n_experts=n_experts),
        grid=(t // tm,),
        in_specs=[
            pl.BlockSpec((tm, d), lambda i: (i, 0)),
            pl.BlockSpec((1, d), lambda i: (0, 0)),
            pl.BlockSpec((d, HEAD_DIM), lambda i: (0, 0)),
        ],
        out_specs=pl.BlockSpec((tm, HEAD_DIM), lambda i: (i, 0)),
        out_shape=jax.ShapeDtypeStruct((t, HEAD_DIM), F32),
        compiler_params=_cparams("parallel"),
        name="router",
    )(x, g.reshape(1, d), w_pad)


def _row_copy(src_hbm, dst_ref, sem, src_row, dst_row):
    return pltpu.make_async_copy(src_hbm.at[pl.ds(src_row, 1)], dst_ref.at[pl.ds(dst_row, 1)], sem)


def _gather_rows_kernel(idx_ref, x_hbm, o_ref, sem, *, rows):
    def start(r, _):
        _row_copy(x_hbm, o_ref, sem, idx_ref[0, r], r).start()
        return 0

    def wait(r, _):
        _row_copy(x_hbm, o_ref, sem, 0, r).wait()
        return 0

    lax.fori_loop(0, rows, start, 0)
    lax.fori_loop(0, rows, wait, 0)


def _gather_rows(x, row_token, *, rows):
    r = row_token.shape[0]
    d = x.shape[1]
    return pl.pallas_call(
        functools.partial(_gather_rows_kernel, rows=rows),
        grid=(r // rows,),
        in_specs=[
            pl.BlockSpec((None, 1, rows), lambda i: (i, 0, 0), memory_space=pltpu.SMEM),
            pl.BlockSpec(memory_space=pl.ANY),
        ],
        out_specs=pl.BlockSpec((rows, d), lambda i: (i, 0)),
        out_shape=jax.ShapeDtypeStruct((r, d), x.dtype),
        scratch_shapes=[pltpu.SemaphoreType.DMA(())],
        compiler_params=_cparams("arbitrary"),
        name="gather_expert_rows",
    )(row_token.reshape(r // rows, 1, rows), x)


def _combine_kernel(pos_ref, x_ref, route_ref, y_hbm, o_ref, ya_ref, yb_ref, sem, *, rows):
    def start(r, _):
        _row_copy(y_hbm, ya_ref, sem, pos_ref[0, r], r).start()
        _row_copy(y_hbm, yb_ref, sem, pos_ref[0, rows + r], r).start()
        return 0

    def wait(r, _):
        _row_copy(y_hbm, ya_ref, sem, 0, r).wait()
        _row_copy(y_hbm, yb_ref, sem, 0, r).wait()
        return 0

    lax.fori_loop(0, rows, start, 0)
    lax.fori_loop(0, rows, wait, 0)
    route = route_ref[...]
    o_ref[...] = x_ref[...] + route[:, 2:3] * ya_ref[...] + route[:, 3:4] * yb_ref[...]


def _combine(x, route, y_sorted, pos, *, rows):
    t, d = x.shape
    return pl.pallas_call(
        functools.partial(_combine_kernel, rows=rows),
        grid=(t // rows,),
        in_specs=[
            pl.BlockSpec((None, 1, TOP_K * rows), lambda i: (i, 0, 0), memory_space=pltpu.SMEM),
            pl.BlockSpec((rows, d), lambda i: (i, 0)),
            pl.BlockSpec((rows, HEAD_DIM), lambda i: (i, 0)),
            pl.BlockSpec(memory_space=pl.ANY),
        ],
        out_specs=pl.BlockSpec((rows, d), lambda i: (i, 0)),
        out_shape=jax.ShapeDtypeStruct((t, d), F32),
        scratch_shapes=[pltpu.VMEM((rows, d), F32), pltpu.VMEM((rows, d), F32), pltpu.SemaphoreType.DMA(())],
        compiler_params=_cparams("arbitrary"),
        name="combine_expert_rows",
    )(pos, x, route, y_sorted)


def _routing_tables(route, n_experts, *, tm, rows):
    t = route.shape[0]
    expert = route[:, :TOP_K].astype(jnp.int32).reshape(-1)
    onehot = (expert[:, None] == jnp.arange(n_experts)[None, :]).astype(jnp.int32)
    rank = jnp.sum((jnp.cumsum(onehot, axis=0) - onehot) * onehot, axis=1)
    count = jnp.sum(onehot, axis=0)
    tiles = (count + tm - 1) // tm
    tile_end = jnp.cumsum(tiles)
    start = (tile_end - tiles) * tm
    dest = start[expert] + rank
    n_rows = (t * TOP_K // tm + n_experts) * tm
    n_tiles = n_rows // tm
    row_token = jnp.zeros((n_rows,), jnp.int32).at[dest].set(jnp.arange(t * TOP_K, dtype=jnp.int32) // TOP_K)
    tile_id = jnp.arange(n_tiles)
    tile_valid = (tile_id < tile_end[-1]).astype(jnp.int32)
    tile_expert = jnp.minimum(jnp.sum(tile_id[:, None] >= tile_end[None, :], axis=1), n_experts - 1)
    last_expert = tile_expert[jnp.maximum(tile_end[-1] - 1, 0)]
    tile_expert = jnp.where(tile_valid > 0, tile_expert, last_expert).astype(jnp.int32)
    dest = dest.reshape(t // rows, rows, TOP_K)
    pos = jnp.concatenate([dest[:, :, 0], dest[:, :, 1]], axis=1).reshape(t // rows, 1, TOP_K * rows)
    return row_token, tile_expert, tile_valid, pos


def _moe(x, g, w_router, wg, wu, wd, *, tm_route, tm, tf, rows):
    n_experts = w_router.shape[1]
    route = _router(x, g, w_router, tm=tm_route)
    row_token, tile_expert, tile_valid, pos = _routing_tables(route, n_experts, tm=tm, rows=rows)
    xs = _gather_rows(x, row_token, rows=rows)
    ys = _expert_ffn(xs, g, wg, wu, wd, tile_expert, tile_valid, tm=tm, tf=tf)
    return _combine(x, route, ys, pos, rows=rows)


def _pick(n, want):
    t = min(n, want)
    while n % t:
        t -= 8
    return t


def kernel(x_prompt, x_sample, cache_sb_k, cache_sb_v, state_ret, page_table, norm_attn, w_in, qnorm_g, knorm_g,
           sb_bias, w_pa, w_pb, w_o, norm_ffn, w_ff_gate, w_ff_up, w_ff_down, w_router, w_exp_gate, w_exp_up,
           w_exp_down):
    batch, seq, d = x_prompt.shape
    dec_batch, dec_seq, _ = x_sample.shape
    depth = w_in.shape[0]
    n_pages = page_table.shape[1]
    past_len = n_pages * cache_sb_k.shape[2]
    n_prompt = batch * seq
    n_sample = dec_batch * dec_seq
    t = n_prompt + n_sample
    log_gamma = jnp.log1p(-jnp.exp2(-5.0 - jnp.arange(N_HEADS, dtype=F32)))

    tm = _pick(int(np.gcd(seq, n_sample)), 1024)
    tq = _pick(seq, 256)
    chunk = _pick(seq, 128)
    rope_tab = _rope_table(seq, dec_seq, past_len, tm)

    x = jnp.concatenate([x_prompt.reshape(n_prompt, d), x_sample.reshape(n_sample, d)], axis=0)
    kp, vp, sp, ks, vs, ss = [], [], [], [], [], []
    for l in range(depth):
        p = _in_projection(x, norm_attn[l], w_in[l], rope_tab, qnorm_g[l], knorm_g[l],
                           n_prompt=n_prompt, seq=seq, tm=tm, tn=512)
        heads = lambda rows, sec, n: rows[:, sec * HEAD_DIM:sec * HEAD_DIM + WIDTH].reshape(n, -1, N_HEADS, HEAD_DIM)
        kp.append(heads(p[:n_prompt], SEC_SK, batch))
        vp.append(heads(p[:n_prompt], SEC_SV, batch))
        ks.append(heads(p[n_prompt:], SEC_SK, dec_batch))
        vs.append(heads(p[n_prompt:], SEC_SV, dec_batch))

        o_sb_p = _sb_prompt(p, sb_bias[l], batch=batch, seq=seq, tq=tq)
        o_sb_s = _sb_decode(p, ks[-1], vs[-1], cache_sb_k, cache_sb_v, page_table, sb_bias[l],
                            layer=l, n_prompt=n_prompt, n_group=_pick(n_pages * 8, 32) // 8)
        ret_p, state_p = _ret_prompt(p, log_gamma, batch=batch, seq=seq, chunk=chunk)
        ret_s, state_s = _ret_sample(p, state_ret[l], log_gamma, n_prompt=n_prompt, dec_seq=dec_seq)
        sp.append(state_p)
        ss.append(state_s)

        o_sb = jnp.concatenate([o_sb_p, o_sb_s.astype(BF16)], axis=0)
        ret = jnp.concatenate([ret_p, ret_s.astype(BF16)], axis=0)
        m = _merge(o_sb, ret, w_pa[l], w_pb[l], p, tm=tm, tn=512)
        x = _out_proj(m, w_o[l], x, tm=tm, tn=512)

        i = l // 2
        if l % 2 == 0:
            x = _dense_ffn(x, norm_ffn[l], w_ff_gate[i], w_ff_up[i], w_ff_down[i], tm=tm, tf=256)
        else:
            x = _moe(x, norm_ffn[l], w_router[i], w_exp_gate[i], w_exp_up[i], w_exp_down[i],
                     tm_route=_pick(t, 512), tm=tm, tf=256, rows=_pick(t, 128))

    return (x[:n_prompt].reshape(batch, seq, d), x[n_prompt:].reshape(dec_batch, dec_seq, d),
            jnp.stack(kp), jnp.stack(vp), jnp.stack(sp), jnp.stack(ks), jnp.stack(vs), jnp.stack(ss))
```

```python
import functools

import jax
import jax.numpy as jnp
import numpy as np
from jax import lax
from jax.experimental import pallas as pl
from jax.experimental.pallas import tpu as pltpu

F32 = jnp.float32
BF16 = jnp.bfloat16

HEAD_DIM = 128
N_HEADS = 8
WIDTH = N_HEADS * HEAD_DIM
PAGE = 128
TOP_K = 2
EPS = 1e-6
ROPE_BASE = 10000.0
NEG_INF = float("-inf")

SEC_SQ, SEC_SK, SEC_SV, SEC_RQ, SEC_RK, SEC_RV, SEC_RG = (s * N_HEADS for s in range(7))
COL_GA = 7 * WIDTH
D_IN_SECTIONS = 7

VMEM_LIMIT = 56 * 1024 * 1024


def _cparams(*semantics):
    return pltpu.CompilerParams(dimension_semantics=semantics, vmem_limit_bytes=VMEM_LIMIT)


def _dot(a, b):
    return jnp.dot(a, b, preferred_element_type=F32)


def _dot_nt(a, b):
    return lax.dot_general(a, b, (((1,), (1,)), ((), ())), preferred_element_type=F32)


def _sigmoid(x):
    return 1.0 / (1.0 + jnp.exp(-x))


def _rms_rows(x):
    return x * lax.rsqrt(jnp.mean(x * x, axis=-1, keepdims=True) + EPS)


def _inproj_kernel(x_ref, g_ref, w_ref, rope_ref, qg_ref, kg_ref, o_ref, h_ref, *, tn):
    j = pl.program_id(1)

    @pl.when(j == 0)
    def _():
        h_ref[...] = (_rms_rows(x_ref[...]) * g_ref[...]).astype(BF16)

    acc = _dot(h_ref[...], w_ref[...])
    sec = j // (WIDTH // tn)
    heads = tn // HEAD_DIM

    @pl.when(sec <= 1)
    def _():
        gain = jnp.where(sec == 0, qg_ref[...], kg_ref[...])
        for hh in range(heads):
            sl = slice(hh * HEAD_DIM, (hh + 1) * HEAD_DIM)
            o_ref[:, sl] = _rms_rows(acc[:, sl]) * gain

    @pl.when(jnp.logical_or(sec == 3, sec == 4))
    def _():
        cos = rope_ref[:, :HEAD_DIM]
        sin = rope_ref[:, HEAD_DIM:]
        scale = jnp.where(sec == 4, HEAD_DIM ** -0.5, 1.0).astype(F32)
        for hh in range(heads):
            sl = slice(hh * HEAD_DIM, (hh + 1) * HEAD_DIM)
            blk = acc[:, sl]
            o_ref[:, sl] = (blk * cos + pltpu.roll(blk, HEAD_DIM // 2, 1) * sin) * scale

    @pl.when(jnp.logical_and(sec != 0, jnp.logical_and(sec != 1, jnp.logical_and(sec != 3, sec != 4))))
    def _():
        o_ref[...] = acc


def _rope_table(seq, dec_seq, past_len, tm):
    half = HEAD_DIM // 2
    inv_freq = ROPE_BASE ** (-jnp.arange(half, dtype=F32) / half)
    pos = jnp.concatenate([jnp.arange(seq), past_len + (jnp.arange(tm) % dec_seq)]).astype(F32)
    ang = pos[:, None] * inv_freq[None, :]
    cos, sin = jnp.cos(ang), jnp.sin(ang)
    return jnp.concatenate([cos, cos, -sin, sin], axis=-1)


def _in_projection(x, g, w, rope_tab, qg, kg, *, n_prompt, seq, tm, tn):
    t, d = x.shape
    d_in = w.shape[1]
    assert t % tm == 0 and d_in % tn == 0 and WIDTH % tn == 0 and seq % tm == 0 and n_prompt % tm == 0
    n_prompt_tiles = n_prompt // tm
    per_seq = seq // tm

    def rope_map(i, j):
        return (jnp.where(i < n_prompt_tiles, i % per_seq, per_seq), 0)

    return pl.pallas_call(
        functools.partial(_inproj_kernel, tn=tn),
        grid=(t // tm, d_in // tn),
        in_specs=[
            pl.BlockSpec((tm, d), lambda i, j: (i, 0)),
            pl.BlockSpec((1, d), lambda i, j: (0, 0)),
            pl.BlockSpec((d, tn), lambda i, j: (0, j)),
            pl.BlockSpec((tm, 2 * HEAD_DIM), rope_map),
            pl.BlockSpec((1, HEAD_DIM), lambda i, j: (0, 0)),
            pl.BlockSpec((1, HEAD_DIM), lambda i, j: (0, 0)),
        ],
        out_specs=pl.BlockSpec((tm, tn), lambda i, j: (i, j)),
        out_shape=jax.ShapeDtypeStruct((t, d_in), F32),
        scratch_shapes=[pltpu.VMEM((tm, d), BF16)],
        compiler_params=_cparams("parallel", "arbitrary"),
        name="in_projection",
    )(x, g.reshape(1, d), w, rope_tab, qg.reshape(1, HEAD_DIM), kg.reshape(1, HEAD_DIM))


LOG2E = 1.4426950408889634
SB_SCALE2 = HEAD_DIM ** -0.5 * LOG2E


def _sb_block(z2, valid, tri):
    soft = jnp.log2(1.0 + jnp.exp2(-jnp.abs(z2)))
    log_beta = jnp.minimum(z2, 0.0) - soft
    log_1m = -jnp.maximum(z2, 0.0) - soft
    if valid is not None:
        log_1m = jnp.where(valid, log_1m, 0.0)
    hi = log_1m.astype(BF16)
    lo = (log_1m - hi.astype(F32)).astype(BF16)
    logw = log_beta + (_dot(hi, tri) + _dot(lo, tri))
    if valid is not None:
        logw = jnp.where(valid, logw, NEG_INF)
    return logw, jnp.sum(log_1m, axis=-1, keepdims=True)


def _tri(n):
    idx = jnp.arange(n)
    return (idx[:, None] > idx[None, :]).astype(BF16)


def _sb_prompt_kernel(bias_ref, q_ref, k_ref, v_ref, tri_ref, o_ref, kb_ref, vb_ref, *, tq, hg):
    h0 = pl.program_id(1) * hg
    qi = pl.program_id(2)

    @pl.when(qi == 0)
    def _():
        kb_ref[...] = k_ref[...].astype(BF16)
        vb_ref[...] = v_ref[...].astype(BF16)

    lanes = [slice(j * HEAD_DIM, (j + 1) * HEAD_DIM) for j in range(hg)]
    q = [q_ref[:, sl].astype(BF16) for sl in lanes]
    bias2 = [bias_ref[h0 + j] * LOG2E for j in range(hg)]
    tri = tri_ref[...]
    row = lax.broadcasted_iota(jnp.int32, (tq, tq), 0)
    col = lax.broadcasted_iota(jnp.int32, (tq, tq), 1)

    def keys(ref, kb, j):
        return ref[pl.ds(pl.multiple_of(kb * tq, tq), tq), lanes[j]]

    def scores(kb, valid):
        return tuple(_sb_block(_dot_nt(q[j], keys(kb_ref, kb, j)) * SB_SCALE2 + bias2[j], valid, tri)
                     for j in range(hg))

    def attend(kb, blk, state):
        new = []
        for j in range(hg):
            (logw, rowsum), (carry, acc) = blk[j], state[j]
            a = jnp.exp2(logw + carry).astype(BF16)
            new.append((carry + rowsum, acc + _dot(a, keys(vb_ref, kb, j))))
        return tuple(new)

    def body(it, carried):
        blk, state = carried
        return scores(qi - 1 - it, None), attend(qi - it, blk, state)

    state = tuple((jnp.zeros((tq, 1), F32), jnp.zeros((tq, HEAD_DIM), F32)) for _ in range(hg))
    blk = scores(qi, col < row)
    blk, state = lax.fori_loop(0, qi, body, (blk, state))
    state = attend(0, blk, state)
    for j in range(hg):
        o_ref[:, lanes[j]] = state[j][1].astype(o_ref.dtype)


def _sb_prompt(p, bias, *, batch, seq, tq, hg):
    nq = seq // tq
    wide = hg * HEAD_DIM
    return pl.pallas_call(
        functools.partial(_sb_prompt_kernel, tq=tq, hg=hg),
        grid_spec=pltpu.PrefetchScalarGridSpec(
            num_scalar_prefetch=0,
            grid=(batch, N_HEADS // hg, nq),
            in_specs=[
                pl.BlockSpec(memory_space=pltpu.SMEM),
                pl.BlockSpec((tq, wide), lambda b, h, qi: (b * nq + qi, SEC_SQ // hg + h)),
                pl.BlockSpec((seq, wide), lambda b, h, qi: (b, SEC_SK // hg + h)),
                pl.BlockSpec((seq, wide), lambda b, h, qi: (b, SEC_SV // hg + h)),
                pl.BlockSpec((tq, tq), lambda b, h, qi: (0, 0)),
            ],
            out_specs=pl.BlockSpec((tq, wide), lambda b, h, qi: (b * nq + qi, h)),
            scratch_shapes=[pltpu.VMEM((seq, wide), BF16), pltpu.VMEM((seq, wide), BF16)],
        ),
        out_shape=jax.ShapeDtypeStruct((batch * seq, WIDTH), BF16),
        compiler_params=_cparams("parallel", "parallel", "arbitrary"),
        name="sb_prompt",
    )(bias, p, p, p, _tri(tq))


def _sb_decode_kernel(pt_ref, q_ref, kn_ref, vn_ref, *rest, n_group, dec_seq):
    k_refs = rest[:n_group]
    v_refs = rest[n_group:2 * n_group]
    bias_ref, tri_ref, o_ref, kpad_ref, vpad_ref, carry_ref, acc_ref = rest[2 * n_group:]
    g = pl.program_id(1)
    rows = N_HEADS * dec_seq
    tri = tri_ref[...]
    bias2 = bias_ref[...] * LOG2E
    qs = [q_ref[:, h * HEAD_DIM:(h + 1) * HEAD_DIM].astype(BF16) for h in range(N_HEADS)]

    def head_rows(ref, h):
        return ref[pl.ds(h, PAGE, stride=N_HEADS), :].astype(BF16)

    def process(pages, valid):
        n = len(pages)
        z = jnp.concatenate([_dot_nt(qs[h], head_rows(k_ref, h)) for k_ref, _ in pages for h in range(N_HEADS)],
                            axis=0)
        logw, rowsum = _sb_block(z * SB_SCALE2 + jnp.concatenate([bias2] * n, axis=0), valid, tri)
        carry = carry_ref[...]
        carries = []
        for r in range(n):
            carries.append(carry)
            carry = carry + rowsum[r * rows:(r + 1) * rows]
        carry_ref[...] = carry
        a = jnp.exp2(logw + jnp.concatenate(carries, axis=0))
        for h in range(N_HEADS):
            sl = slice(h * HEAD_DIM, (h + 1) * HEAD_DIM)
            acc = acc_ref[:, sl]
            for r, (_, v_ref) in enumerate(pages):
                ah = a[r * rows + h * dec_seq:r * rows + (h + 1) * dec_seq, :].astype(BF16)
                acc = acc + _dot(ah, head_rows(v_ref, h))
            acc_ref[:, sl] = acc

    @pl.when(g == 0)
    def _():
        carry_ref[...] = jnp.zeros_like(carry_ref)
        acc_ref[...] = jnp.zeros_like(acc_ref)
        kpad_ref[...] = jnp.zeros_like(kpad_ref)
        vpad_ref[...] = jnp.zeros_like(vpad_ref)
        kpad_ref[:rows] = kn_ref[...]
        vpad_ref[:rows] = vn_ref[...]
        t_idx = lax.broadcasted_iota(jnp.int32, (rows, PAGE), 0) % dec_seq
        s_idx = lax.broadcasted_iota(jnp.int32, (rows, PAGE), 1)
        process([(kpad_ref, vpad_ref)], s_idx < t_idx)

    process([(k_refs[r], v_refs[r]) for r in range(n_group - 1, -1, -1)], None)

    @pl.when(g == pl.num_programs(1) - 1)
    def _():
        o_ref[...] = acc_ref[...]


def _sb_decode(p, k_new, v_new, cache_k, cache_v, page_table, bias, *, layer, n_prompt, n_group):
    dec_batch, dec_seq = k_new.shape[:2]
    n_pages = page_table.shape[1]
    assert n_pages % n_group == 0 and dec_seq == 8
    n_steps = n_pages // n_group
    rows = N_HEADS * dec_seq
    q_row0 = n_prompt // dec_seq
    page_rows = PAGE * N_HEADS
    cache_k = cache_k.reshape(cache_k.shape[:2] + (page_rows, HEAD_DIM))
    cache_v = cache_v.reshape(cache_v.shape[:2] + (page_rows, HEAD_DIM))
    k_new = k_new.reshape(dec_batch, rows, HEAD_DIM)
    v_new = v_new.reshape(dec_batch, rows, HEAD_DIM)

    def page_map(r):
        def index(b, g, pt):
            return (layer, pt[b * n_pages + n_pages - (g + 1) * n_group + r], 0, 0)
        return index

    page_specs = [pl.BlockSpec((None, None, page_rows, HEAD_DIM), page_map(r)) for r in range(n_group)]
    new_spec = pl.BlockSpec((None, rows, HEAD_DIM), lambda b, g, pt: (b, 0, 0))
    bias_rows = jnp.broadcast_to(jnp.repeat(bias.astype(F32), dec_seq)[:, None], (rows, PAGE))
    return pl.pallas_call(
        functools.partial(_sb_decode_kernel, n_group=n_group, dec_seq=dec_seq),
        grid_spec=pltpu.PrefetchScalarGridSpec(
            num_scalar_prefetch=1,
            grid=(dec_batch, n_steps),
            in_specs=[pl.BlockSpec((dec_seq, WIDTH), lambda b, g, pt: (q_row0 + b, SEC_SQ // N_HEADS)),
                      new_spec, new_spec] + page_specs + page_specs + [
                pl.BlockSpec((rows, PAGE), lambda b, g, pt: (0, 0)),
                pl.BlockSpec((PAGE, PAGE), lambda b, g, pt: (0, 0)),
            ],
            out_specs=pl.BlockSpec((dec_seq, WIDTH), lambda b, g, pt: (b, 0)),
            scratch_shapes=[
                pltpu.VMEM((page_rows, HEAD_DIM), F32),
                pltpu.VMEM((page_rows, HEAD_DIM), F32),
                pltpu.VMEM((rows, 1), F32),
                pltpu.VMEM((dec_seq, WIDTH), F32),
            ],
        ),
        out_shape=jax.ShapeDtypeStruct((dec_batch * dec_seq, WIDTH), F32),
        compiler_params=_cparams("parallel", "arbitrary"),
        name="sb_decode",
    )(page_table.reshape(-1), p, k_new, v_new, *([cache_k] * n_group), *([cache_v] * n_group),
      bias_rows, _tri(PAGE))


def _ret_consts(log_gamma, chunk):
    idx = jnp.arange(chunk, dtype=F32)
    diff = idx[:, None] - idx[None, :]
    causal = diff >= 0
    decay = jnp.where(causal[None], jnp.exp(jnp.where(causal, diff, 0.0)[None] * log_gamma[:, None, None]), 0.0)
    q_decay = jnp.exp((idx + 1.0)[None, :] * log_gamma[:, None])
    k_decay = jnp.exp((chunk - 1.0 - idx)[None, :] * log_gamma[:, None])
    lanes = (N_HEADS, chunk, HEAD_DIM)
    return (decay, jnp.broadcast_to(q_decay[:, :, None], lanes), jnp.broadcast_to(k_decay[:, :, None], lanes),
            jnp.exp(chunk * log_gamma))


def _ret_chunk(q, k, v, rg, state, decay, qd, kd, state_decay):
    qb, kb, vb = q.astype(BF16), k.astype(BF16), v.astype(BF16)
    scores = _dot_nt(qb, kb) * decay
    o = _dot(scores.astype(BF16), vb) + _dot(qb, state.astype(BF16)) * qd
    new_state = state_decay * state + lax.dot_general((k * kd).astype(BF16), vb, (((0,), (0,)), ((), ())),
                                                      preferred_element_type=F32)
    out = _rms_rows(o) * (rg * _sigmoid(rg))
    return out, new_state


def _ret_prompt_kernel(sd_ref, q_ref, k_ref, v_ref, rg_ref, decay_ref, qd_ref, kd_ref, o_ref, s_ref, *, chunk, n_chunks):
    h = pl.program_id(1)
    s_ref[...] = jnp.zeros_like(s_ref)

    def body(c, _):
        sl = pl.ds(pl.multiple_of(c * chunk, chunk), chunk)
        out, new_state = _ret_chunk(q_ref[sl, :], k_ref[sl, :], v_ref[sl, :], rg_ref[sl, :], s_ref[...],
                                    decay_ref[...], qd_ref[...], kd_ref[...], sd_ref[h])
        s_ref[...] = new_state
        o_ref[sl, :] = out.astype(o_ref.dtype)
        return 0

    lax.fori_loop(0, n_chunks, body, 0)


def _ret_prompt(p, log_gamma, *, batch, seq, chunk):
    decay, qd, kd, sd = _ret_consts(log_gamma, chunk)
    col = lambda sec: pl.BlockSpec((seq, HEAD_DIM), lambda b, h: (b, sec + h))
    per_head = lambda n: pl.BlockSpec((None, chunk, n), lambda b, h: (h, 0, 0))
    return pl.pallas_call(
        functools.partial(_ret_prompt_kernel, chunk=chunk, n_chunks=seq // chunk),
        grid=(batch, N_HEADS),
        in_specs=[pl.BlockSpec(memory_space=pltpu.SMEM), col(SEC_RQ), col(SEC_RK), col(SEC_RV), col(SEC_RG),
                  per_head(chunk), per_head(HEAD_DIM), per_head(HEAD_DIM)],
        out_specs=[pl.BlockSpec((seq, HEAD_DIM), lambda b, h: (b, h)),
                   pl.BlockSpec((None, None, HEAD_DIM, HEAD_DIM), lambda b, h: (b, h, 0, 0))],
        out_shape=[jax.ShapeDtypeStruct((batch * seq, WIDTH), BF16),
                   jax.ShapeDtypeStruct((batch, N_HEADS, HEAD_DIM, HEAD_DIM), F32)],
        compiler_params=_cparams("parallel", "parallel"),
        name="retention_prompt",
    )(sd, p, p, p, p, decay, qd, kd)


def _ret_sample_kernel(sd_ref, q_ref, k_ref, v_ref, rg_ref, s_in_ref, decay_ref, qd_ref, kd_ref, o_ref, s_out_ref):
    for h in range(N_HEADS):
        sl = slice(h * HEAD_DIM, (h + 1) * HEAD_DIM)
        out, new_state = _ret_chunk(q_ref[:, sl], k_ref[:, sl], v_ref[:, sl], rg_ref[:, sl], s_in_ref[h],
                                    decay_ref[h], qd_ref[h], kd_ref[h], sd_ref[h])
        o_ref[:, sl] = out
        s_out_ref[h] = new_state


def _ret_sample(p, state, log_gamma, *, n_prompt, dec_seq):
    dec_batch = state.shape[0]
    decay, qd, kd, sd = _ret_consts(log_gamma, dec_seq)
    row0 = n_prompt // dec_seq
    col = lambda sec: pl.BlockSpec((dec_seq, WIDTH), lambda b: (row0 + b, sec // N_HEADS))
    full = lambda a: pl.BlockSpec(a.shape, lambda b: (0,) * a.ndim)
    state_spec = pl.BlockSpec((None, N_HEADS, HEAD_DIM, HEAD_DIM), lambda b: (b, 0, 0, 0))
    return pl.pallas_call(
        _ret_sample_kernel,
        grid=(dec_batch,),
        in_specs=[pl.BlockSpec(memory_space=pltpu.SMEM), col(SEC_RQ), col(SEC_RK), col(SEC_RV), col(SEC_RG),
                  state_spec, full(decay), full(qd), full(kd)],
        out_specs=[pl.BlockSpec((dec_seq, WIDTH), lambda b: (b, 0)), state_spec],
        out_shape=[jax.ShapeDtypeStruct((dec_batch * dec_seq, WIDTH), F32),
                   jax.ShapeDtypeStruct(state.shape, F32)],
        compiler_params=_cparams("parallel"),
        name="retention_sample",
    )(sd, p, p, p, p, state, decay, qd, kd)


def _merge_kernel(a_ref, b_ref, wa_ref, wb_ref, ga_ref, gb_ref, o_ref):
    ya = _dot(a_ref[...], wa_ref[...])
    yb = _dot(b_ref[...], wb_ref[...])
    o_ref[...] = (_sigmoid(ga_ref[...]) * ya + _sigmoid(gb_ref[...]) * yb).astype(o_ref.dtype)


def _merge(o_sb, ret, w_pa, w_pb, p, *, tm, tn):
    t = o_sb.shape[0]
    d = w_pa.shape[1]
    ga0 = COL_GA // tn
    gb0 = (COL_GA + d) // tn
    return pl.pallas_call(
        _merge_kernel,
        grid=(t // tm, d // tn),
        in_specs=[
            pl.BlockSpec((tm, WIDTH), lambda i, j: (i, 0)),
            pl.BlockSpec((tm, WIDTH), lambda i, j: (i, 0)),
            pl.BlockSpec((WIDTH, tn), lambda i, j: (0, j)),
            pl.BlockSpec((WIDTH, tn), lambda i, j: (0, j)),
            pl.BlockSpec((tm, tn), lambda i, j: (i, ga0 + j)),
            pl.BlockSpec((tm, tn), lambda i, j: (i, gb0 + j)),
        ],
        out_specs=pl.BlockSpec((tm, tn), lambda i, j: (i, j)),
        out_shape=jax.ShapeDtypeStruct((t, d), BF16),
        compiler_params=_cparams("parallel", "arbitrary"),
        name="branch_merge",
    )(o_sb, ret, w_pa, w_pb, p, p)


def _out_proj_kernel(m_ref, w_ref, x_ref, o_ref):
    o_ref[...] = x_ref[...] + _dot(m_ref[...], w_ref[...])


def _out_proj(m, w_o, x, *, tm, tn):
    t, d = x.shape
    return pl.pallas_call(
        _out_proj_kernel,
        grid=(t // tm, d // tn),
        in_specs=[
            pl.BlockSpec((tm, d), lambda i, j: (i, 0)),
            pl.BlockSpec((d, tn), lambda i, j: (0, j)),
            pl.BlockSpec((tm, tn), lambda i, j: (i, j)),
        ],
        out_specs=pl.BlockSpec((tm, tn), lambda i, j: (i, j)),
        out_shape=jax.ShapeDtypeStruct((t, d), F32),
        compiler_params=_cparams("parallel", "arbitrary"),
        name="out_projection",
    )(m, w_o, x)


def _swiglu(h, wg, wu, wd):
    gate = _dot(h, wg)
    up = _dot(h, wu)
    return _dot((gate * _sigmoid(gate) * up).astype(BF16), wd)


def _dense_ffn_kernel(x_ref, g_ref, wg_ref, wu_ref, wd_ref, o_ref, h_ref):
    @pl.when(pl.program_id(1) == 0)
    def _():
        x = x_ref[...]
        h_ref[...] = (_rms_rows(x) * g_ref[...]).astype(BF16)
        o_ref[...] = x

    o_ref[...] += _swiglu(h_ref[...], wg_ref[...], wu_ref[...], wd_ref[...])


def _dense_ffn(x, g, wg, wu, wd, *, tm, tf):
    t, d = x.shape
    d_ff = wg.shape[1]
    return pl.pallas_call(
        _dense_ffn_kernel,
        grid=(t // tm, d_ff // tf),
        in_specs=[
            pl.BlockSpec((tm, d), lambda i, f: (i, 0), pipeline_mode=pl.Buffered(1)),
            pl.BlockSpec((1, d), lambda i, f: (0, 0)),
            pl.BlockSpec((d, tf), lambda i, f: (0, f)),
            pl.BlockSpec((d, tf), lambda i, f: (0, f)),
            pl.BlockSpec((tf, d), lambda i, f: (f, 0)),
        ],
        out_specs=pl.BlockSpec((tm, d), lambda i, f: (i, 0)),
        out_shape=jax.ShapeDtypeStruct((t, d), F32),
        scratch_shapes=[pltpu.VMEM((tm, d), BF16)],
        compiler_params=_cparams("parallel", "arbitrary"),
        name="dense_ffn",
    )(x, g.reshape(1, d), wg, wu, wd)


def _expert_ffn_kernel(te_ref, tr_ref, h_ref, wg_ref, wu_ref, wd_ref, o_ref, wgb_ref, wub_ref, wdb_ref, *, sub):
    i = pl.program_id(0)
    n_sub = (tr_ref[i] + (sub - 1)) // sub

    @pl.when(pl.program_id(1) == 0)
    def _():
        o_ref[...] = jnp.zeros_like(o_ref)

    @pl.when(n_sub > 0)
    def _():
        wgb_ref[...] = wg_ref[...].astype(BF16)
        wub_ref[...] = wu_ref[...].astype(BF16)
        wdb_ref[...] = wd_ref[...].astype(BF16)

        def body(s, _):
            rs = pl.ds(pl.multiple_of(s * sub, sub), sub)
            o_ref[rs, :] += _swiglu(h_ref[rs, :], wgb_ref[...], wub_ref[...], wdb_ref[...])
            return 0

        lax.fori_loop(0, n_sub, body, 0)


def _expert_ffn(hs, wg, wu, wd, tile_expert, tile_rows, *, tm, tf, sub):
    r, d = hs.shape
    d_ff = wg.shape[2]
    nf = d_ff // tf

    def f_idx(i, f, tr):
        return jnp.where(tr[i] > 0, f, nf - 1)

    return pl.pallas_call(
        functools.partial(_expert_ffn_kernel, sub=sub),
        grid_spec=pltpu.PrefetchScalarGridSpec(
            num_scalar_prefetch=2,
            grid=(r // tm, nf),
            in_specs=[
                pl.BlockSpec((tm, d), lambda i, f, te, tr: (i, 0), pipeline_mode=pl.Buffered(1)),
                pl.BlockSpec((None, d, tf), lambda i, f, te, tr: (te[i], 0, f_idx(i, f, tr))),
                pl.BlockSpec((None, d, tf), lambda i, f, te, tr: (te[i], 0, f_idx(i, f, tr))),
                pl.BlockSpec((None, tf, d), lambda i, f, te, tr: (te[i], f_idx(i, f, tr), 0)),
            ],
            out_specs=pl.BlockSpec((tm, d), lambda i, f, te, tr: (i, 0)),
            scratch_shapes=[pltpu.VMEM((d, tf), BF16), pltpu.VMEM((d, tf), BF16), pltpu.VMEM((tf, d), BF16)],
        ),
        out_shape=jax.ShapeDtypeStruct((r, d), F32),
        compiler_params=_cparams("parallel", "arbitrary"),
        name="expert_ffn",
    )(tile_expert, tile_rows, hs, wg, wu, wd)


def _router_kernel(x_ref, g_ref, w_ref, o_ref, *, n_experts):
    h = _rms_rows(x_ref[...]) * g_ref[...]
    w = w_ref[...]
    h_hi = h.astype(BF16)
    h_lo = (h - h_hi.astype(F32)).astype(BF16)
    w_hi = w.astype(BF16)
    w_lo = (w - w_hi.astype(F32)).astype(BF16)
    logits = _dot(h_hi, w_hi) + (_dot(h_hi, w_lo) + _dot(h_lo, w_hi))
    lane = lax.broadcasted_iota(jnp.int32, logits.shape, 1).astype(F32)
    lg = jnp.where(lane < n_experts, logits, NEG_INF)
    m1 = jnp.max(lg, axis=-1, keepdims=True)
    i1 = jnp.min(jnp.where(lg == m1, lane, float(HEAD_DIM)), axis=-1, keepdims=True)
    lg2 = jnp.where(lane == i1, NEG_INF, lg)
    m2 = jnp.max(lg2, axis=-1, keepdims=True)
    i2 = jnp.min(jnp.where(lg2 == m2, lane, float(HEAD_DIM)), axis=-1, keepdims=True)
    e = jnp.exp(m2 - m1)
    g1 = 1.0 / (1.0 + e)
    g2 = e / (1.0 + e)
    o_ref[...] = jnp.where(lane == 0, i1, jnp.where(lane == 1, i2, jnp.where(lane == 2, g1, jnp.where(lane == 3, g2, 0.0))))


def _router(x, g, w_router, *, tm):
    t, d = x.shape
    n_experts = w_router.shape[1]
    w_pad = jnp.pad(w_router, ((0, 0), (0, HEAD_DIM - n_experts)))
    return pl.pallas_call(
        functools.partial(_router_kernel, n_experts=n_experts),
        grid=(t // tm,),
        in_specs=[
            pl.BlockSpec((tm, d), lambda i: (i, 0)),
            pl.BlockSpec((1, d), lambda i: (0, 0)),
            pl.BlockSpec((d, HEAD_DIM), lambda i: (0, 0)),
        ],
        out_specs=pl.BlockSpec((tm, HEAD_DIM), lambda i: (i, 0)),
        out_shape=jax.ShapeDtypeStruct((t, HEAD_DIM), F32),
        compiler_params=_cparams("parallel"),
        name="router",
    )(x, g.reshape(1, d), w_pad)


def _row_copy(src_hbm, dst_ref, sem, src_row, dst_row):
    return pltpu.make_async_copy(src_hbm.at[pl.ds(src_row, 1)], dst_ref.at[pl.ds(dst_row, 1)], sem)


def _gather_rows_kernel(idx_ref, g_ref, x_hbm, o_ref, buf_ref, sem, *, rows):
    def start(r, _):
        _row_copy(x_hbm, buf_ref, sem, idx_ref[0, r], r).start()
        return 0

    def wait(r, _):
        _row_copy(x_hbm, buf_ref, sem, 0, r).wait()
        return 0

    lax.fori_loop(0, rows, start, 0, unroll=8)
    lax.fori_loop(0, rows, wait, 0, unroll=8)
    o_ref[...] = (_rms_rows(buf_ref[...]) * g_ref[...]).astype(BF16)


def _gather_rows(x, g, row_token, *, rows):
    r = row_token.shape[0]
    d = x.shape[1]
    return pl.pallas_call(
        functools.partial(_gather_rows_kernel, rows=rows),
        grid=(r // rows,),
        in_specs=[
            pl.BlockSpec((None, 1, rows), lambda i: (i, 0, 0), memory_space=pltpu.SMEM),
            pl.BlockSpec((1, d), lambda i: (0, 0)),
            pl.BlockSpec(memory_space=pl.ANY),
        ],
        out_specs=pl.BlockSpec((rows, d), lambda i: (i, 0)),
        out_shape=jax.ShapeDtypeStruct((r, d), BF16),
        scratch_shapes=[pltpu.VMEM((rows, d), F32), pltpu.SemaphoreType.DMA(())],
        compiler_params=_cparams("arbitrary"),
        name="gather_expert_rows",
    )(row_token.reshape(r // rows, 1, rows), g.reshape(1, d), x)


def _combine_kernel(pos_ref, x_ref, route_ref, y_hbm, o_ref, ya_ref, yb_ref, sem, *, rows):
    def start(r, _):
        _row_copy(y_hbm, ya_ref, sem, pos_ref[0, r], r).start()
        _row_copy(y_hbm, yb_ref, sem, pos_ref[0, rows + r], r).start()
        return 0

    def wait(r, _):
        _row_copy(y_hbm, ya_ref, sem, 0, r).wait()
        _row_copy(y_hbm, yb_ref, sem, 0, r).wait()
        return 0

    lax.fori_loop(0, rows, start, 0, unroll=8)
    lax.fori_loop(0, rows, wait, 0, unroll=8)
    route = route_ref[...]
    o_ref[...] = x_ref[...] + route[:, 2:3] * ya_ref[...] + route[:, 3:4] * yb_ref[...]


def _combine(x, route, y_sorted, pos, *, rows):
    t, d = x.shape
    return pl.pallas_call(
        functools.partial(_combine_kernel, rows=rows),
        grid=(t // rows,),
        in_specs=[
            pl.BlockSpec((None, 1, TOP_K * rows), lambda i: (i, 0, 0), memory_space=pltpu.SMEM),
            pl.BlockSpec((rows, d), lambda i: (i, 0)),
            pl.BlockSpec((rows, HEAD_DIM), lambda i: (i, 0)),
            pl.BlockSpec(memory_space=pl.ANY),
        ],
        out_specs=pl.BlockSpec((rows, d), lambda i: (i, 0)),
        out_shape=jax.ShapeDtypeStruct((t, d), F32),
        scratch_shapes=[pltpu.VMEM((rows, d), F32), pltpu.VMEM((rows, d), F32), pltpu.SemaphoreType.DMA(())],
        compiler_params=_cparams("arbitrary"),
        name="combine_expert_rows",
    )(pos, x, route, y_sorted)


def _routing_tables(route, n_experts, *, tm, rows):
    t = route.shape[0]
    expert = route[:, :TOP_K].astype(jnp.int32).reshape(-1)
    onehot = (expert[:, None] == jnp.arange(n_experts)[None, :]).astype(jnp.int32)
    rank = jnp.sum((jnp.cumsum(onehot, axis=0) - onehot) * onehot, axis=1)
    count = jnp.sum(onehot, axis=0)
    tiles = (count + tm - 1) // tm
    tile_end = jnp.cumsum(tiles)
    start = (tile_end - tiles) * tm
    dest = start[expert] + rank
    n_rows = (t * TOP_K // tm + n_experts) * tm
    n_tiles = n_rows // tm
    row_token = jnp.zeros((n_rows,), jnp.int32).at[dest].set(jnp.arange(t * TOP_K, dtype=jnp.int32) // TOP_K)
    tile_id = jnp.arange(n_tiles)
    tile_valid = tile_id < tile_end[-1]
    tile_expert = jnp.minimum(jnp.sum(tile_id[:, None] >= tile_end[None, :], axis=1), n_experts - 1)
    last_expert = tile_expert[jnp.maximum(tile_end[-1] - 1, 0)]
    first_tile = (tile_end - tiles)[tile_expert]
    tile_rows = jnp.clip(count[tile_expert] - (tile_id - first_tile) * tm, 0, tm)
    tile_rows = jnp.where(tile_valid, tile_rows, 0).astype(jnp.int32)
    tile_expert = jnp.where(tile_valid, tile_expert, last_expert).astype(jnp.int32)
    dest = dest.reshape(t // rows, rows, TOP_K)
    pos = jnp.concatenate([dest[:, :, 0], dest[:, :, 1]], axis=1).reshape(t // rows, 1, TOP_K * rows)
    return row_token, tile_expert, tile_rows, pos


def _moe(x, g, w_router, wg, wu, wd, *, tm_route, tm, tf, sub, rows):
    n_experts = w_router.shape[1]
    route = _router(x, g, w_router, tm=tm_route)
    row_token, tile_expert, tile_rows, pos = _routing_tables(route, n_experts, tm=tm, rows=rows)
    hs = _gather_rows(x, g, row_token, rows=rows)
    ys = _expert_ffn(hs, wg, wu, wd, tile_expert, tile_rows, tm=tm, tf=tf, sub=sub)
    return _combine(x, route, ys, pos, rows=rows)


def _pick(n, want):
    t = min(n, want)
    while n % t:
        t -= 8
    return t


def kernel(x_prompt, x_sample, cache_sb_k, cache_sb_v, state_ret, page_table, norm_attn, w_in, qnorm_g, knorm_g,
           sb_bias, w_pa, w_pb, w_o, norm_ffn, w_ff_gate, w_ff_up, w_ff_down, w_router, w_exp_gate, w_exp_up,
           w_exp_down):
    batch, seq, d = x_prompt.shape
    dec_batch, dec_seq, _ = x_sample.shape
    depth = w_in.shape[0]
    n_pages = page_table.shape[1]
    past_len = n_pages * cache_sb_k.shape[2]
    n_prompt = batch * seq
    n_sample = dec_batch * dec_seq
    t = n_prompt + n_sample
    log_gamma = jnp.log1p(-jnp.exp2(-5.0 - jnp.arange(N_HEADS, dtype=F32)))

    tm = _pick(int(np.gcd(seq, n_sample)), 1024)
    tq = _pick(seq, 256)
    chunk = _pick(seq, 128)
    tm_moe = _pick(TOP_K * t, 1024)
    rope_tab = _rope_table(seq, dec_seq, past_len, tm)
    w_in, w_pa, w_pb, w_o, w_ff_gate, w_ff_up, w_ff_down = (
        w.astype(BF16) for w in (w_in, w_pa, w_pb, w_o, w_ff_gate, w_ff_up, w_ff_down))

    x = jnp.concatenate([x_prompt.reshape(n_prompt, d), x_sample.reshape(n_sample, d)], axis=0)
    kp, vp, sp, ks, vs, ss = [], [], [], [], [], []
    for l in range(depth):
        p = _in_projection(x, norm_attn[l], w_in[l], rope_tab, qnorm_g[l], knorm_g[l],
                           n_prompt=n_prompt, seq=seq, tm=tm, tn=1024)
        heads = lambda rows, sec, n: rows[:, sec * HEAD_DIM:sec * HEAD_DIM + WIDTH].reshape(n, -1, N_HEADS, HEAD_DIM)
        kp.append(heads(p[:n_prompt], SEC_SK, batch))
        vp.append(heads(p[:n_prompt], SEC_SV, batch))
        ks.append(heads(p[n_prompt:], SEC_SK, dec_batch))
        vs.append(heads(p[n_prompt:], SEC_SV, dec_batch))

        o_sb_p = _sb_prompt(p, sb_bias[l], batch=batch, seq=seq, tq=tq, hg=4)
        o_sb_s = _sb_decode(p, ks[-1], vs[-1], cache_sb_k, cache_sb_v, page_table, sb_bias[l],
                            layer=l, n_prompt=n_prompt, n_group=_pick(n_pages * 8, 64) // 8)
        ret_p, state_p = _ret_prompt(p, log_gamma, batch=batch, seq=seq, chunk=chunk)
        ret_s, state_s = _ret_sample(p, state_ret[l], log_gamma, n_prompt=n_prompt, dec_seq=dec_seq)
        sp.append(state_p)
        ss.append(state_s)

        o_sb = jnp.concatenate([o_sb_p, o_sb_s.astype(BF16)], axis=0)
        ret = jnp.concatenate([ret_p, ret_s.astype(BF16)], axis=0)
        m = _merge(o_sb, ret, w_pa[l], w_pb[l], p, tm=tm, tn=1024)
        x = _out_proj(m, w_o[l], x, tm=tm, tn=1024)

        i = l // 2
        if l % 2 == 0:
            x = _dense_ffn(x, norm_ffn[l], w_ff_gate[i], w_ff_up[i], w_ff_down[i], tm=tm, tf=512)
        else:
            x = _moe(x, norm_ffn[l], w_router[i], w_exp_gate[i], w_exp_up[i], w_exp_down[i],
                     tm_route=_pick(t, 512), tm=tm_moe, tf=512, sub=_pick(tm_moe, 256), rows=_pick(tm_moe, 128))

    return (x[:n_prompt].reshape(batch, seq, d), x[n_prompt:].reshape(dec_batch, dec_seq, d),
            jnp.stack(kp), jnp.stack(vp), jnp.stack(sp), jnp.stack(ks), jnp.stack(vs), jnp.stack(ss))
```

```python
import functools

import jax
import jax.numpy as jnp
import numpy as np
from jax import lax
from jax.experimental import pallas as pl
from jax.experimental.pallas import tpu as pltpu

F32 = jnp.float32
BF16 = jnp.bfloat16

HEAD_DIM = 128
N_HEADS = 8
WIDTH = N_HEADS * HEAD_DIM
PAGE = 128
TOP_K = 2
EPS = 1e-6
ROPE_BASE = 10000.0
NEG_INF = float("-inf")

SEC_SQ, SEC_SK, SEC_SV, SEC_RQ, SEC_RK, SEC_RV, SEC_RG = (s * N_HEADS for s in range(7))
COL_GA = 7 * WIDTH
D_IN_SECTIONS = 7

VMEM_LIMIT = 56 * 1024 * 1024


def _cparams(*semantics):
    return pltpu.CompilerParams(dimension_semantics=semantics, vmem_limit_bytes=VMEM_LIMIT)


def _dot(a, b):
    return jnp.dot(a, b, preferred_element_type=F32)


def _dot_nt(a, b):
    return lax.dot_general(a, b, (((1,), (1,)), ((), ())), preferred_element_type=F32)


def _sigmoid(x):
    return 1.0 / (1.0 + jnp.exp(-x))


def _rms_rows(x):
    return x * lax.rsqrt(jnp.mean(x * x, axis=-1, keepdims=True) + EPS)


def _inproj_kernel(x_ref, g_ref, w_ref, rope_ref, qg_ref, kg_ref, o_ref, h_ref, *, tn):
    j = pl.program_id(1)

    @pl.when(j == 0)
    def _():
        h_ref[...] = (_rms_rows(x_ref[...]) * g_ref[...]).astype(BF16)

    acc = _dot(h_ref[...], w_ref[...])
    sec = j // (WIDTH // tn)
    heads = tn // HEAD_DIM

    @pl.when(sec <= 1)
    def _():
        gain = jnp.where(sec == 0, qg_ref[...], kg_ref[...])
        for hh in range(heads):
            sl = slice(hh * HEAD_DIM, (hh + 1) * HEAD_DIM)
            o_ref[:, sl] = _rms_rows(acc[:, sl]) * gain

    @pl.when(jnp.logical_or(sec == 3, sec == 4))
    def _():
        cos = rope_ref[:, :HEAD_DIM]
        sin = rope_ref[:, HEAD_DIM:]
        scale = jnp.where(sec == 4, HEAD_DIM ** -0.5, 1.0).astype(F32)
        for hh in range(heads):
            sl = slice(hh * HEAD_DIM, (hh + 1) * HEAD_DIM)
            blk = acc[:, sl]
            o_ref[:, sl] = (blk * cos + pltpu.roll(blk, HEAD_DIM // 2, 1) * sin) * scale

    @pl.when(jnp.logical_and(sec != 0, jnp.logical_and(sec != 1, jnp.logical_and(sec != 3, sec != 4))))
    def _():
        o_ref[...] = acc


def _rope_table(seq, dec_seq, past_len, tm):
    half = HEAD_DIM // 2
    inv_freq = ROPE_BASE ** (-jnp.arange(half, dtype=F32) / half)
    pos = jnp.concatenate([jnp.arange(seq), past_len + (jnp.arange(tm) % dec_seq)]).astype(F32)
    ang = pos[:, None] * inv_freq[None, :]
    cos, sin = jnp.cos(ang), jnp.sin(ang)
    return jnp.concatenate([cos, cos, -sin, sin], axis=-1)


def _in_projection(x, g, w, rope_tab, qg, kg, *, n_prompt, seq, tm, tn):
    t, d = x.shape
    d_in = w.shape[1]
    assert t % tm == 0 and d_in % tn == 0 and WIDTH % tn == 0 and seq % tm == 0 and n_prompt % tm == 0
    n_prompt_tiles = n_prompt // tm
    per_seq = seq // tm

    def rope_map(i, j):
        return (jnp.where(i < n_prompt_tiles, i % per_seq, per_seq), 0)

    return pl.pallas_call(
        functools.partial(_inproj_kernel, tn=tn),
        grid=(t // tm, d_in // tn),
        in_specs=[
            pl.BlockSpec((tm, d), lambda i, j: (i, 0)),
            pl.BlockSpec((1, d), lambda i, j: (0, 0)),
            pl.BlockSpec((d, tn), lambda i, j: (0, j)),
            pl.BlockSpec((tm, 2 * HEAD_DIM), rope_map),
            pl.BlockSpec((1, HEAD_DIM), lambda i, j: (0, 0)),
            pl.BlockSpec((1, HEAD_DIM), lambda i, j: (0, 0)),
        ],
        out_specs=pl.BlockSpec((tm, tn), lambda i, j: (i, j)),
        out_shape=jax.ShapeDtypeStruct((t, d_in), F32),
        scratch_shapes=[pltpu.VMEM((tm, d), BF16)],
        compiler_params=_cparams("parallel", "arbitrary"),
        name="in_projection",
    )(x, g.reshape(1, d), w, rope_tab, qg.reshape(1, HEAD_DIM), kg.reshape(1, HEAD_DIM))


LOG2E = 1.4426950408889634
SB_SCALE2 = HEAD_DIM ** -0.5 * LOG2E


def _sb_block(z2, valid, tri):
    soft = jnp.log2(1.0 + jnp.exp2(-jnp.abs(z2)))
    log_beta = jnp.minimum(z2, 0.0) - soft
    log_1m = -jnp.maximum(z2, 0.0) - soft
    if valid is not None:
        log_1m = jnp.where(valid, log_1m, 0.0)
    hi = log_1m.astype(BF16)
    lo = (log_1m - hi.astype(F32)).astype(BF16)
    logw = log_beta + (_dot(hi, tri) + _dot(lo, tri))
    if valid is not None:
        logw = jnp.where(valid, logw, NEG_INF)
    return logw, jnp.sum(log_1m, axis=-1, keepdims=True)


def _tri(n):
    idx = jnp.arange(n)
    return (idx[:, None] > idx[None, :]).astype(BF16)


def _sb_prompt_kernel(bias_ref, q_ref, k_ref, v_ref, tri_ref, o_ref, kb_ref, vb_ref, *, tq, hg):
    h0 = pl.program_id(1) * hg
    qi = pl.program_id(2)

    @pl.when(qi == 0)
    def _():
        kb_ref[...] = k_ref[...].astype(BF16)
        vb_ref[...] = v_ref[...].astype(BF16)

    lanes = [slice(j * HEAD_DIM, (j + 1) * HEAD_DIM) for j in range(hg)]
    q = [q_ref[:, sl].astype(BF16) for sl in lanes]
    bias2 = [bias_ref[h0 + j] * LOG2E for j in range(hg)]
    tri = tri_ref[...]
    row = lax.broadcasted_iota(jnp.int32, (tq, tq), 0)
    col = lax.broadcasted_iota(jnp.int32, (tq, tq), 1)

    def keys(ref, kb, j):
        return ref[pl.ds(pl.multiple_of(kb * tq, tq), tq), lanes[j]]

    def scores(kb, valid):
        return tuple(_sb_block(_dot_nt(q[j], keys(kb_ref, kb, j)) * SB_SCALE2 + bias2[j], valid, tri)
                     for j in range(hg))

    def attend(kb, blk, state):
        new = []
        for j in range(hg):
            (logw, rowsum), (carry, acc) = blk[j], state[j]
            a = jnp.exp2(logw + carry).astype(BF16)
            new.append((carry + rowsum, acc + _dot(a, keys(vb_ref, kb, j))))
        return tuple(new)

    def body(it, carried):
        blk, state = carried
        return scores(qi - 1 - it, None), attend(qi - it, blk, state)

    state = tuple((jnp.zeros((tq, 1), F32), jnp.zeros((tq, HEAD_DIM), F32)) for _ in range(hg))
    blk = scores(qi, col < row)
    blk, state = lax.fori_loop(0, qi, body, (blk, state))
    state = attend(0, blk, state)
    for j in range(hg):
        o_ref[:, lanes[j]] = state[j][1].astype(o_ref.dtype)


def _sb_prompt(p, bias, *, batch, seq, tq, hg):
    nq = seq // tq
    wide = hg * HEAD_DIM
    return pl.pallas_call(
        functools.partial(_sb_prompt_kernel, tq=tq, hg=hg),
        grid_spec=pltpu.PrefetchScalarGridSpec(
            num_scalar_prefetch=0,
            grid=(batch, N_HEADS // hg, nq),
            in_specs=[
                pl.BlockSpec(memory_space=pltpu.SMEM),
                pl.BlockSpec((tq, wide), lambda b, h, qi: (b * nq + qi, SEC_SQ // hg + h)),
                pl.BlockSpec((seq, wide), lambda b, h, qi: (b, SEC_SK // hg + h)),
                pl.BlockSpec((seq, wide), lambda b, h, qi: (b, SEC_SV // hg + h)),
                pl.BlockSpec((tq, tq), lambda b, h, qi: (0, 0)),
            ],
            out_specs=pl.BlockSpec((tq, wide), lambda b, h, qi: (b * nq + qi, h)),
            scratch_shapes=[pltpu.VMEM((seq, wide), BF16), pltpu.VMEM((seq, wide), BF16)],
        ),
        out_shape=jax.ShapeDtypeStruct((batch * seq, WIDTH), BF16),
        compiler_params=_cparams("parallel", "parallel", "arbitrary"),
        name="sb_prompt",
    )(bias, p, p, p, _tri(tq))


def _sb_decode_kernel(pt_ref, q_ref, kn_ref, vn_ref, *rest, n_group, dec_seq):
    k_refs = rest[:n_group]
    v_refs = rest[n_group:2 * n_group]
    bias_ref, tri_ref, o_ref, kpad_ref, vpad_ref, carry_ref, acc_ref = rest[2 * n_group:]
    g = pl.program_id(1)
    rows = N_HEADS * dec_seq
    tri = tri_ref[...]
    bias2 = bias_ref[...] * LOG2E
    qs = [q_ref[:, h * HEAD_DIM:(h + 1) * HEAD_DIM].astype(BF16) for h in range(N_HEADS)]

    def head_rows(ref, h):
        return ref[pl.ds(h, PAGE, stride=N_HEADS), :].astype(BF16)

    def process(pages, valid):
        n = len(pages)
        z = jnp.concatenate([_dot_nt(qs[h], head_rows(k_ref, h)) for k_ref, _ in pages for h in range(N_HEADS)],
                            axis=0)
        logw, rowsum = _sb_block(z * SB_SCALE2 + jnp.concatenate([bias2] * n, axis=0), valid, tri)
        carry = carry_ref[...]
        carries = []
        for r in range(n):
            carries.append(carry)
            carry = carry + rowsum[r * rows:(r + 1) * rows]
        carry_ref[...] = carry
        a = jnp.exp2(logw + jnp.concatenate(carries, axis=0))
        for h in range(N_HEADS):
            sl = slice(h * HEAD_DIM, (h + 1) * HEAD_DIM)
            acc = acc_ref[:, sl]
            for r, (_, v_ref) in enumerate(pages):
                ah = a[r * rows + h * dec_seq:r * rows + (h + 1) * dec_seq, :].astype(BF16)
                acc = acc + _dot(ah, head_rows(v_ref, h))
            acc_ref[:, sl] = acc

    @pl.when(g == 0)
    def _():
        carry_ref[...] = jnp.zeros_like(carry_ref)
        acc_ref[...] = jnp.zeros_like(acc_ref)
        kpad_ref[...] = jnp.zeros_like(kpad_ref)
        vpad_ref[...] = jnp.zeros_like(vpad_ref)
        kpad_ref[:rows] = kn_ref[...]
        vpad_ref[:rows] = vn_ref[...]
        t_idx = lax.broadcasted_iota(jnp.int32, (rows, PAGE), 0) % dec_seq
        s_idx = lax.broadcasted_iota(jnp.int32, (rows, PAGE), 1)
        process([(kpad_ref, vpad_ref)], s_idx < t_idx)

    process([(k_refs[r], v_refs[r]) for r in range(n_group - 1, -1, -1)], None)

    @pl.when(g == pl.num_programs(1) - 1)
    def _():
        o_ref[...] = acc_ref[...]


def _sb_decode(p, k_new, v_new, cache_k, cache_v, page_table, bias, *, layer, n_prompt, n_group):
    dec_batch, dec_seq = k_new.shape[:2]
    n_pages = page_table.shape[1]
    assert n_pages % n_group == 0 and dec_seq == 8
    n_steps = n_pages // n_group
    rows = N_HEADS * dec_seq
    q_row0 = n_prompt // dec_seq
    page_rows = PAGE * N_HEADS
    cache_k = cache_k.reshape(cache_k.shape[:2] + (page_rows, HEAD_DIM))
    cache_v = cache_v.reshape(cache_v.shape[:2] + (page_rows, HEAD_DIM))
    k_new = k_new.reshape(dec_batch, rows, HEAD_DIM)
    v_new = v_new.reshape(dec_batch, rows, HEAD_DIM)

    def page_map(r):
        def index(b, g, pt):
            return (layer, pt[b * n_pages + n_pages - (g + 1) * n_group + r], 0, 0)
        return index

    page_specs = [pl.BlockSpec((None, None, page_rows, HEAD_DIM), page_map(r)) for r in range(n_group)]
    new_spec = pl.BlockSpec((None, rows, HEAD_DIM), lambda b, g, pt: (b, 0, 0))
    bias_rows = jnp.broadcast_to(jnp.repeat(bias.astype(F32), dec_seq)[:, None], (rows, PAGE))
    return pl.pallas_call(
        functools.partial(_sb_decode_kernel, n_group=n_group, dec_seq=dec_seq),
        grid_spec=pltpu.PrefetchScalarGridSpec(
            num_scalar_prefetch=1,
            grid=(dec_batch, n_steps),
            in_specs=[pl.BlockSpec((dec_seq, WIDTH), lambda b, g, pt: (q_row0 + b, SEC_SQ // N_HEADS)),
                      new_spec, new_spec] + page_specs + page_specs + [
                pl.BlockSpec((rows, PAGE), lambda b, g, pt: (0, 0)),
                pl.BlockSpec((PAGE, PAGE), lambda b, g, pt: (0, 0)),
            ],
            out_specs=pl.BlockSpec((dec_seq, WIDTH), lambda b, g, pt: (b, 0)),
            scratch_shapes=[
                pltpu.VMEM((page_rows, HEAD_DIM), F32),
                pltpu.VMEM((page_rows, HEAD_DIM), F32),
                pltpu.VMEM((rows, 1), F32),
                pltpu.VMEM((dec_seq, WIDTH), F32),
            ],
        ),
        out_shape=jax.ShapeDtypeStruct((dec_batch * dec_seq, WIDTH), F32),
        compiler_params=_cparams("parallel", "arbitrary"),
        name="sb_decode",
    )(page_table.reshape(-1), p, k_new, v_new, *([cache_k] * n_group), *([cache_v] * n_group),
      bias_rows, _tri(PAGE))


def _ret_consts(log_gamma, chunk):
    idx = jnp.arange(chunk, dtype=F32)
    diff = idx[:, None] - idx[None, :]
    causal = diff >= 0
    decay = jnp.where(causal[None], jnp.exp(jnp.where(causal, diff, 0.0)[None] * log_gamma[:, None, None]), 0.0)
    q_decay = jnp.exp((idx + 1.0)[None, :] * log_gamma[:, None])
    k_decay = jnp.exp((chunk - 1.0 - idx)[None, :] * log_gamma[:, None])
    lanes = (N_HEADS, chunk, HEAD_DIM)
    return (decay, jnp.broadcast_to(q_decay[:, :, None], lanes), jnp.broadcast_to(k_decay[:, :, None], lanes),
            jnp.exp(chunk * log_gamma))


def _ret_chunk(q, k, v, rg, state, decay, qd, kd, state_decay):
    qb, kb, vb = q.astype(BF16), k.astype(BF16), v.astype(BF16)
    scores = _dot_nt(qb, kb) * decay
    o = _dot(scores.astype(BF16), vb) + _dot(qb, state.astype(BF16)) * qd
    new_state = state_decay * state + lax.dot_general((k * kd).astype(BF16), vb, (((0,), (0,)), ((), ())),
                                                      preferred_element_type=F32)
    out = _rms_rows(o) * (rg * _sigmoid(rg))
    return out, new_state


def _ret_prompt_kernel(sd_ref, q_ref, k_ref, v_ref, rg_ref, decay_ref, qd_ref, kd_ref, o_ref, s_ref, *,
                       chunk, n_chunks, hg):
    h0 = pl.program_id(1) * hg
    s_ref[...] = jnp.zeros_like(s_ref)

    def body(c, _):
        rows = pl.ds(pl.multiple_of(c * chunk, chunk), chunk)
        for j in range(hg):
            sl = slice(j * HEAD_DIM, (j + 1) * HEAD_DIM)
            out, new_state = _ret_chunk(q_ref[rows, sl], k_ref[rows, sl], v_ref[rows, sl], rg_ref[rows, sl],
                                        s_ref[j], decay_ref[j], qd_ref[j], kd_ref[j], sd_ref[h0 + j])
            s_ref[j] = new_state
            o_ref[rows, sl] = out.astype(o_ref.dtype)
        return 0

    lax.fori_loop(0, n_chunks, body, 0)


def _ret_prompt(p, log_gamma, *, batch, seq, chunk, hg):
    decay, qd, kd, sd = _ret_consts(log_gamma, chunk)
    wide = hg * HEAD_DIM
    col = lambda sec: pl.BlockSpec((seq, wide), lambda b, h: (b, sec // hg + h))
    per_head = lambda n: pl.BlockSpec((hg, chunk, n), lambda b, h: (h, 0, 0))
    return pl.pallas_call(
        functools.partial(_ret_prompt_kernel, chunk=chunk, n_chunks=seq // chunk, hg=hg),
        grid=(batch, N_HEADS // hg),
        in_specs=[pl.BlockSpec(memory_space=pltpu.SMEM), col(SEC_RQ), col(SEC_RK), col(SEC_RV), col(SEC_RG),
                  per_head(chunk), per_head(HEAD_DIM), per_head(HEAD_DIM)],
        out_specs=[pl.BlockSpec((seq, wide), lambda b, h: (b, h)),
                   pl.BlockSpec((None, hg, HEAD_DIM, HEAD_DIM), lambda b, h: (b, h, 0, 0))],
        out_shape=[jax.ShapeDtypeStruct((batch * seq, WIDTH), BF16),
                   jax.ShapeDtypeStruct((batch, N_HEADS, HEAD_DIM, HEAD_DIM), F32)],
        compiler_params=_cparams("parallel", "parallel"),
        name="retention_prompt",
    )(sd, p, p, p, p, decay, qd, kd)


def _ret_sample_kernel(sd_ref, q_ref, k_ref, v_ref, rg_ref, s_in_ref, decay_ref, qd_ref, kd_ref, o_ref, s_out_ref):
    for h in range(N_HEADS):
        sl = slice(h * HEAD_DIM, (h + 1) * HEAD_DIM)
        out, new_state = _ret_chunk(q_ref[:, sl], k_ref[:, sl], v_ref[:, sl], rg_ref[:, sl], s_in_ref[h],
                                    decay_ref[h], qd_ref[h], kd_ref[h], sd_ref[h])
        o_ref[:, sl] = out
        s_out_ref[h] = new_state


def _ret_sample(p, state, log_gamma, *, n_prompt, dec_seq):
    dec_batch = state.shape[0]
    decay, qd, kd, sd = _ret_consts(log_gamma, dec_seq)
    row0 = n_prompt // dec_seq
    col = lambda sec: pl.BlockSpec((dec_seq, WIDTH), lambda b: (row0 + b, sec // N_HEADS))
    full = lambda a: pl.BlockSpec(a.shape, lambda b: (0,) * a.ndim)
    state_spec = pl.BlockSpec((None, N_HEADS, HEAD_DIM, HEAD_DIM), lambda b: (b, 0, 0, 0))
    return pl.pallas_call(
        _ret_sample_kernel,
        grid=(dec_batch,),
        in_specs=[pl.BlockSpec(memory_space=pltpu.SMEM), col(SEC_RQ), col(SEC_RK), col(SEC_RV), col(SEC_RG),
                  state_spec, full(decay), full(qd), full(kd)],
        out_specs=[pl.BlockSpec((dec_seq, WIDTH), lambda b: (b, 0)), state_spec],
        out_shape=[jax.ShapeDtypeStruct((dec_batch * dec_seq, WIDTH), F32),
                   jax.ShapeDtypeStruct(state.shape, F32)],
        compiler_params=_cparams("parallel"),
        name="retention_sample",
    )(sd, p, p, p, p, state, decay, qd, kd)


def _merge_kernel(a_ref, b_ref, wa_ref, wb_ref, ga_ref, gb_ref, o_ref):
    ya = _dot(a_ref[...], wa_ref[...])
    yb = _dot(b_ref[...], wb_ref[...])
    o_ref[...] = (_sigmoid(ga_ref[...]) * ya + _sigmoid(gb_ref[...]) * yb).astype(o_ref.dtype)


def _merge(o_sb, ret, w_pa, w_pb, p, *, tm, tn):
    t = o_sb.shape[0]
    d = w_pa.shape[1]
    ga0 = COL_GA // tn
    gb0 = (COL_GA + d) // tn
    return pl.pallas_call(
        _merge_kernel,
        grid=(t // tm, d // tn),
        in_specs=[
            pl.BlockSpec((tm, WIDTH), lambda i, j: (i, 0)),
            pl.BlockSpec((tm, WIDTH), lambda i, j: (i, 0)),
            pl.BlockSpec((WIDTH, tn), lambda i, j: (0, j)),
            pl.BlockSpec((WIDTH, tn), lambda i, j: (0, j)),
            pl.BlockSpec((tm, tn), lambda i, j: (i, ga0 + j)),
            pl.BlockSpec((tm, tn), lambda i, j: (i, gb0 + j)),
        ],
        out_specs=pl.BlockSpec((tm, tn), lambda i, j: (i, j)),
        out_shape=jax.ShapeDtypeStruct((t, d), BF16),
        compiler_params=_cparams("parallel", "arbitrary"),
        name="branch_merge",
    )(o_sb, ret, w_pa, w_pb, p, p)


def _out_proj_kernel(m_ref, w_ref, x_ref, o_ref):
    o_ref[...] = x_ref[...] + _dot(m_ref[...], w_ref[...])


def _out_proj(m, w_o, x, *, tm, tn):
    t, d = x.shape
    return pl.pallas_call(
        _out_proj_kernel,
        grid=(t // tm, d // tn),
        in_specs=[
            pl.BlockSpec((tm, d), lambda i, j: (i, 0)),
            pl.BlockSpec((d, tn), lambda i, j: (0, j)),
            pl.BlockSpec((tm, tn), lambda i, j: (i, j)),
        ],
        out_specs=pl.BlockSpec((tm, tn), lambda i, j: (i, j)),
        out_shape=jax.ShapeDtypeStruct((t, d), F32),
        compiler_params=_cparams("parallel", "arbitrary"),
        name="out_projection",
    )(m, w_o, x)


def _swiglu(h, wg, wu, wd):
    gate = _dot(h, wg)
    up = _dot(h, wu)
    return _dot((gate * _sigmoid(gate) * up).astype(BF16), wd)


def _dense_ffn_kernel(x_ref, g_ref, wg_ref, wu_ref, wd_ref, o_ref, h_ref):
    @pl.when(pl.program_id(1) == 0)
    def _():
        x = x_ref[...]
        h_ref[...] = (_rms_rows(x) * g_ref[...]).astype(BF16)
        o_ref[...] = x

    o_ref[...] += _swiglu(h_ref[...], wg_ref[...], wu_ref[...], wd_ref[...])


def _dense_ffn(x, g, wg, wu, wd, *, tm, tf):
    t, d = x.shape
    d_ff = wg.shape[1]
    return pl.pallas_call(
        _dense_ffn_kernel,
        grid=(t // tm, d_ff // tf),
        in_specs=[
            pl.BlockSpec((tm, d), lambda i, f: (i, 0), pipeline_mode=pl.Buffered(1)),
            pl.BlockSpec((1, d), lambda i, f: (0, 0)),
            pl.BlockSpec((d, tf), lambda i, f: (0, f)),
            pl.BlockSpec((d, tf), lambda i, f: (0, f)),
            pl.BlockSpec((tf, d), lambda i, f: (f, 0)),
        ],
        out_specs=pl.BlockSpec((tm, d), lambda i, f: (i, 0)),
        out_shape=jax.ShapeDtypeStruct((t, d), F32),
        scratch_shapes=[pltpu.VMEM((tm, d), BF16)],
        compiler_params=_cparams("parallel", "arbitrary"),
        name="dense_ffn",
    )(x, g.reshape(1, d), wg, wu, wd)


def _expert_ffn_kernel(te_ref, tr_ref, h_ref, wg_ref, wu_ref, wd_ref, o_ref, wgb_ref, wub_ref, wdb_ref, *, sub):
    i = pl.program_id(0)
    n_sub = (tr_ref[i] + (sub - 1)) // sub

    @pl.when(pl.program_id(1) == 0)
    def _():
        o_ref[...] = jnp.zeros_like(o_ref)

    @pl.when(n_sub > 0)
    def _():
        wgb_ref[...] = wg_ref[...].astype(BF16)
        wub_ref[...] = wu_ref[...].astype(BF16)
        wdb_ref[...] = wd_ref[...].astype(BF16)

        def body(s, _):
            rs = pl.ds(pl.multiple_of(s * sub, sub), sub)
            o_ref[rs, :] += _swiglu(h_ref[rs, :], wgb_ref[...], wub_ref[...], wdb_ref[...])
            return 0

        lax.fori_loop(0, n_sub, body, 0)


def _expert_ffn(hs, wg, wu, wd, tile_expert, tile_rows, *, tm, tf, sub):
    r, d = hs.shape
    d_ff = wg.shape[2]
    nf = d_ff // tf

    def f_idx(i, f, tr):
        return jnp.where(tr[i] > 0, f, nf - 1)

    return pl.pallas_call(
        functools.partial(_expert_ffn_kernel, sub=sub),
        grid_spec=pltpu.PrefetchScalarGridSpec(
            num_scalar_prefetch=2,
            grid=(r // tm, nf),
            in_specs=[
                pl.BlockSpec((tm, d), lambda i, f, te, tr: (i, 0), pipeline_mode=pl.Buffered(1)),
                pl.BlockSpec((None, d, tf), lambda i, f, te, tr: (te[i], 0, f_idx(i, f, tr))),
                pl.BlockSpec((None, d, tf), lambda i, f, te, tr: (te[i], 0, f_idx(i, f, tr))),
                pl.BlockSpec((None, tf, d), lambda i, f, te, tr: (te[i], f_idx(i, f, tr), 0)),
            ],
            out_specs=pl.BlockSpec((tm, d), lambda i, f, te, tr: (i, 0)),
            scratch_shapes=[pltpu.VMEM((d, tf), BF16), pltpu.VMEM((d, tf), BF16), pltpu.VMEM((tf, d), BF16)],
        ),
        out_shape=jax.ShapeDtypeStruct((r, d), F32),
        compiler_params=_cparams("parallel", "arbitrary"),
        name="expert_ffn",
    )(tile_expert, tile_rows, hs, wg, wu, wd)


def _router_kernel(x_ref, g_ref, w_ref, o_ref, hn_ref, *, n_experts):
    h = _rms_rows(x_ref[...]) * g_ref[...]
    tm, d = h.shape
    n_col = d // HEAD_DIM
    for c in range(n_col):
        hn_ref[pl.ds(c, tm, stride=n_col), :] = h[:, c * HEAD_DIM:(c + 1) * HEAD_DIM]
    w = w_ref[...]
    h_hi = h.astype(BF16)
    h_lo = (h - h_hi.astype(F32)).astype(BF16)
    w_hi = w.astype(BF16)
    w_lo = (w - w_hi.astype(F32)).astype(BF16)
    logits = _dot(h_hi, w_hi) + (_dot(h_hi, w_lo) + _dot(h_lo, w_hi))
    lane = lax.broadcasted_iota(jnp.int32, logits.shape, 1).astype(F32)
    lg = jnp.where(lane < n_experts, logits, NEG_INF)
    m1 = jnp.max(lg, axis=-1, keepdims=True)
    i1 = jnp.min(jnp.where(lg == m1, lane, float(HEAD_DIM)), axis=-1, keepdims=True)
    lg2 = jnp.where(lane == i1, NEG_INF, lg)
    m2 = jnp.max(lg2, axis=-1, keepdims=True)
    i2 = jnp.min(jnp.where(lg2 == m2, lane, float(HEAD_DIM)), axis=-1, keepdims=True)
    e = jnp.exp(m2 - m1)
    g1 = 1.0 / (1.0 + e)
    g2 = e / (1.0 + e)
    o_ref[...] = jnp.where(lane == 0, i1, jnp.where(lane == 1, i2, jnp.where(lane == 2, g1, jnp.where(lane == 3, g2, 0.0))))


def _router(x, g, w_router, *, tm):
    t, d = x.shape
    n_experts = w_router.shape[1]
    w_pad = jnp.pad(w_router, ((0, 0), (0, HEAD_DIM - n_experts)))
    return pl.pallas_call(
        functools.partial(_router_kernel, n_experts=n_experts),
        grid=(t // tm,),
        in_specs=[
            pl.BlockSpec((tm, d), lambda i: (i, 0)),
            pl.BlockSpec((1, d), lambda i: (0, 0)),
            pl.BlockSpec((d, HEAD_DIM), lambda i: (0, 0)),
        ],
        out_specs=[pl.BlockSpec((tm, HEAD_DIM), lambda i: (i, 0)),
                   pl.BlockSpec((tm * (d // HEAD_DIM), HEAD_DIM), lambda i: (i, 0))],
        out_shape=[jax.ShapeDtypeStruct((t, HEAD_DIM), F32),
                   jax.ShapeDtypeStruct((t * (d // HEAD_DIM), HEAD_DIM), F32)],
        compiler_params=_cparams("parallel"),
        name="router",
    )(x, g.reshape(1, d), w_pad)


def _row_copy(src_hbm, dst_ref, sem, src_row, dst_row):
    return pltpu.make_async_copy(src_hbm.at[pl.ds(src_row, 1)], dst_ref.at[pl.ds(dst_row, 1)], sem)


def _slab_copy(src_hbm, dst_ref, sem, src_token, dst_token, n_col):
    return pltpu.make_async_copy(src_hbm.at[pl.ds(pl.multiple_of(src_token * n_col, n_col), n_col)],
                                 dst_ref.at[pl.ds(pl.multiple_of(dst_token * n_col, n_col), n_col)], sem)


def _gather_rows_kernel(idx_ref, hn_hbm, o_ref, buf_ref, sem, *, rows, n_col):
    def start(r, _):
        _slab_copy(hn_hbm, buf_ref, sem, idx_ref[0, r], r, n_col).start()
        return 0

    def wait(r, _):
        _slab_copy(hn_hbm, buf_ref, sem, 0, r, n_col).wait()
        return 0

    lax.fori_loop(0, rows, start, 0, unroll=8)
    lax.fori_loop(0, rows, wait, 0, unroll=8)
    for c in range(n_col):
        o_ref[:, c * HEAD_DIM:(c + 1) * HEAD_DIM] = buf_ref[pl.ds(c, rows, stride=n_col), :].astype(BF16)


def _gather_rows(hn, row_token, *, rows, d):
    r = row_token.shape[0]
    n_col = d // HEAD_DIM
    return pl.pallas_call(
        functools.partial(_gather_rows_kernel, rows=rows, n_col=n_col),
        grid=(r // rows,),
        in_specs=[
            pl.BlockSpec((None, 1, rows), lambda i: (i, 0, 0), memory_space=pltpu.SMEM),
            pl.BlockSpec(memory_space=pl.ANY),
        ],
        out_specs=pl.BlockSpec((rows, d), lambda i: (i, 0)),
        out_shape=jax.ShapeDtypeStruct((r, d), BF16),
        scratch_shapes=[pltpu.VMEM((rows * n_col, HEAD_DIM), F32), pltpu.SemaphoreType.DMA(())],
        compiler_params=_cparams("arbitrary"),
        name="gather_expert_rows",
    )(row_token.reshape(r // rows, 1, rows), hn)


def _combine_kernel(pos_ref, x_ref, route_ref, y_hbm, op_ref, os_ref, ya_ref, yb_ref, sem, *, rows, prompt_steps):
    def start(r, _):
        _row_copy(y_hbm, ya_ref, sem, pos_ref[0, r], r).start()
        _row_copy(y_hbm, yb_ref, sem, pos_ref[0, rows + r], r).start()
        return 0

    def wait(r, _):
        _row_copy(y_hbm, ya_ref, sem, 0, r).wait()
        _row_copy(y_hbm, yb_ref, sem, 0, r).wait()
        return 0

    lax.fori_loop(0, rows, start, 0, unroll=8)
    lax.fori_loop(0, rows, wait, 0, unroll=8)
    route = route_ref[...]
    out = x_ref[...] + route[:, 2:3] * ya_ref[...] + route[:, 3:4] * yb_ref[...]
    i = pl.program_id(0)

    @pl.when(i < prompt_steps)
    def _():
        op_ref[...] = out

    @pl.when(i >= prompt_steps)
    def _():
        os_ref[...] = out


def _combine(x, route, y_sorted, pos, *, rows, n_prompt):
    t, d = x.shape
    assert n_prompt % rows == 0 and 0 < n_prompt < t
    prompt_steps = n_prompt // rows
    return pl.pallas_call(
        functools.partial(_combine_kernel, rows=rows, prompt_steps=prompt_steps),
        grid=(t // rows,),
        in_specs=[
            pl.BlockSpec((None, 1, TOP_K * rows), lambda i: (i, 0, 0), memory_space=pltpu.SMEM),
            pl.BlockSpec((rows, d), lambda i: (i, 0)),
            pl.BlockSpec((rows, HEAD_DIM), lambda i: (i, 0)),
            pl.BlockSpec(memory_space=pl.ANY),
        ],
        out_specs=[pl.BlockSpec((rows, d), lambda i: (jnp.minimum(i, prompt_steps - 1), 0)),
                   pl.BlockSpec((rows, d), lambda i: (jnp.maximum(i - prompt_steps, 0), 0))],
        out_shape=[jax.ShapeDtypeStruct((n_prompt, d), F32), jax.ShapeDtypeStruct((t - n_prompt, d), F32)],
        scratch_shapes=[pltpu.VMEM((rows, d), F32), pltpu.VMEM((rows, d), F32), pltpu.SemaphoreType.DMA(())],
        compiler_params=_cparams("arbitrary"),
        name="combine_expert_rows",
    )(pos, x, route, y_sorted)


def _routing_tables(route, n_experts, *, tm, rows):
    t = route.shape[0]
    expert = route[:, :TOP_K].astype(jnp.int32).reshape(-1)
    onehot = (expert[:, None] == jnp.arange(n_experts)[None, :]).astype(jnp.int32)
    rank = jnp.sum((jnp.cumsum(onehot, axis=0) - onehot) * onehot, axis=1)
    count = jnp.sum(onehot, axis=0)
    tiles = (count + tm - 1) // tm
    tile_end = jnp.cumsum(tiles)
    start = (tile_end - tiles) * tm
    dest = start[expert] + rank
    n_rows = (t * TOP_K // tm + n_experts) * tm
    n_tiles = n_rows // tm
    row_token = jnp.zeros((n_rows,), jnp.int32).at[dest].set(jnp.arange(t * TOP_K, dtype=jnp.int32) // TOP_K)
    tile_id = jnp.arange(n_tiles)
    tile_valid = tile_id < tile_end[-1]
    tile_expert = jnp.minimum(jnp.sum(tile_id[:, None] >= tile_end[None, :], axis=1), n_experts - 1)
    last_expert = tile_expert[jnp.maximum(tile_end[-1] - 1, 0)]
    first_tile = (tile_end - tiles)[tile_expert]
    tile_rows = jnp.clip(count[tile_expert] - (tile_id - first_tile) * tm, 0, tm)
    tile_rows = jnp.where(tile_valid, tile_rows, 0).astype(jnp.int32)
    tile_expert = jnp.where(tile_valid, tile_expert, last_expert).astype(jnp.int32)
    dest = dest.reshape(t // rows, rows, TOP_K)
    pos = jnp.concatenate([dest[:, :, 0], dest[:, :, 1]], axis=1).reshape(t // rows, 1, TOP_K * rows)
    return row_token, tile_expert, tile_rows, pos


def _moe(x, g, w_router, wg, wu, wd, *, n_prompt, tm_route, tm, tf, sub, rows):
    n_experts = w_router.shape[1]
    route, hn = _router(x, g, w_router, tm=tm_route)
    row_token, tile_expert, tile_rows, pos = _routing_tables(route, n_experts, tm=tm, rows=rows)
    hs = _gather_rows(hn, row_token, rows=rows, d=x.shape[1])
    ys = _expert_ffn(hs, wg, wu, wd, tile_expert, tile_rows, tm=tm, tf=tf, sub=sub)
    return _combine(x, route, ys, pos, rows=rows, n_prompt=n_prompt)


def _kv_rows_kernel(*refs, depth, tr, prompt_steps):
    ins, (kp_ref, vp_ref, ks_ref, vs_ref) = refs[:2 * depth], refs[2 * depth:]
    l = pl.program_id(0)
    i = pl.program_id(1)

    def emit(src_ref, dst_ref):
        for h in range(N_HEADS):
            dst_ref[pl.ds(h, tr, stride=N_HEADS), :] = src_ref[:, h * HEAD_DIM:(h + 1) * HEAD_DIM]

    for layer in range(depth):
        @pl.when(jnp.logical_and(l == layer, i < prompt_steps))
        def _():
            emit(ins[2 * layer], kp_ref)
            emit(ins[2 * layer + 1], vp_ref)

        @pl.when(jnp.logical_and(l == layer, i >= prompt_steps))
        def _():
            emit(ins[2 * layer], ks_ref)
            emit(ins[2 * layer + 1], vs_ref)


def _kv_rows(projections, *, n_prompt, tr):
    depth = len(projections)
    t = projections[0].shape[0]
    assert n_prompt % tr == 0 and (t - n_prompt) % tr == 0
    steps = t // tr
    prompt_steps = n_prompt // tr

    def src(layer, sec):
        def index(l, i):
            return (jnp.where(l == layer, i, jnp.where(l < layer, 0, steps - 1)), sec // N_HEADS)
        return pl.BlockSpec((tr, WIDTH), index)

    prompt_spec = pl.BlockSpec((None, tr * N_HEADS, HEAD_DIM), lambda l, i: (l, jnp.minimum(i, prompt_steps - 1), 0))
    sample_spec = pl.BlockSpec((None, tr * N_HEADS, HEAD_DIM), lambda l, i: (l, jnp.maximum(i - prompt_steps, 0), 0))
    prompt_shape = jax.ShapeDtypeStruct((depth, n_prompt * N_HEADS, HEAD_DIM), F32)
    sample_shape = jax.ShapeDtypeStruct((depth, (t - n_prompt) * N_HEADS, HEAD_DIM), F32)
    return pl.pallas_call(
        functools.partial(_kv_rows_kernel, depth=depth, tr=tr, prompt_steps=prompt_steps),
        grid=(depth, steps),
        in_specs=[src(layer, sec) for layer in range(depth) for sec in (SEC_SK, SEC_SV)],
        out_specs=[prompt_spec, prompt_spec, sample_spec, sample_spec],
        out_shape=[prompt_shape, prompt_shape, sample_shape, sample_shape],
        compiler_params=_cparams("arbitrary", "arbitrary"),
        name="kv_rows",
    )(*[p for p in projections for _ in range(2)])


def _pick(n, want):
    t = min(n, want)
    while n % t:
        t -= 8
    return t


def kernel(x_prompt, x_sample, cache_sb_k, cache_sb_v, state_ret, page_table, norm_attn, w_in, qnorm_g, knorm_g,
           sb_bias, w_pa, w_pb, w_o, norm_ffn, w_ff_gate, w_ff_up, w_ff_down, w_router, w_exp_gate, w_exp_up,
           w_exp_down):
    batch, seq, d = x_prompt.shape
    dec_batch, dec_seq, _ = x_sample.shape
    depth = w_in.shape[0]
    n_pages = page_table.shape[1]
    past_len = n_pages * cache_sb_k.shape[2]
    n_prompt = batch * seq
    n_sample = dec_batch * dec_seq
    t = n_prompt + n_sample
    log_gamma = jnp.log1p(-jnp.exp2(-5.0 - jnp.arange(N_HEADS, dtype=F32)))

    tm = _pick(int(np.gcd(seq, n_sample)), 1024)
    tq = _pick(seq, 256)
    chunk = _pick(seq, 128)
    tm_moe = _pick(TOP_K * t, 1024)
    rope_tab = _rope_table(seq, dec_seq, past_len, tm)
    w_in, w_pa, w_pb, w_o, w_ff_gate, w_ff_up, w_ff_down = (
        w.astype(BF16) for w in (w_in, w_pa, w_pb, w_o, w_ff_gate, w_ff_up, w_ff_down))

    x = jnp.concatenate([x_prompt.reshape(n_prompt, d), x_sample.reshape(n_sample, d)], axis=0)
    projections, sp, ss = [], [], []
    y_prompt = y_sample = None
    for l in range(depth):
        p = _in_projection(x, norm_attn[l], w_in[l], rope_tab, qnorm_g[l], knorm_g[l],
                           n_prompt=n_prompt, seq=seq, tm=tm, tn=1024)
        projections.append(p)
        new_rows = lambda sec: p[n_prompt:, sec * HEAD_DIM:sec * HEAD_DIM + WIDTH].reshape(
            dec_batch, dec_seq, N_HEADS, HEAD_DIM)

        o_sb_p = _sb_prompt(p, sb_bias[l], batch=batch, seq=seq, tq=tq, hg=4)
        o_sb_s = _sb_decode(p, new_rows(SEC_SK), new_rows(SEC_SV), cache_sb_k, cache_sb_v, page_table, sb_bias[l],
                            layer=l, n_prompt=n_prompt, n_group=_pick(n_pages * 8, 64) // 8)
        ret_p, state_p = _ret_prompt(p, log_gamma, batch=batch, seq=seq, chunk=chunk, hg=4)
        ret_s, state_s = _ret_sample(p, state_ret[l], log_gamma, n_prompt=n_prompt, dec_seq=dec_seq)
        sp.append(state_p)
        ss.append(state_s)

        o_sb = jnp.concatenate([o_sb_p, o_sb_s.astype(BF16)], axis=0)
        ret = jnp.concatenate([ret_p, ret_s.astype(BF16)], axis=0)
        m = _merge(o_sb, ret, w_pa[l], w_pb[l], p, tm=tm, tn=1024)
        x = _out_proj(m, w_o[l], x, tm=tm, tn=1024)

        i = l // 2
        if l % 2 == 0:
            x = _dense_ffn(x, norm_ffn[l], w_ff_gate[i], w_ff_up[i], w_ff_down[i], tm=tm, tf=512)
        else:
            y_prompt, y_sample = _moe(x, norm_ffn[l], w_router[i], w_exp_gate[i], w_exp_up[i], w_exp_down[i],
                                      n_prompt=n_prompt, tm_route=_pick(t, 512), tm=tm_moe, tf=512,
                                      sub=_pick(tm_moe, 256), rows=_pick(int(np.gcd(tm_moe, n_sample)), 128))
            if l + 1 < depth:
                x = jnp.concatenate([y_prompt, y_sample], axis=0)
    if depth % 2:
        y_prompt, y_sample = x[:n_prompt], x[n_prompt:]

    kp, vp, ks, vs = _kv_rows(projections, n_prompt=n_prompt, tr=_pick(int(np.gcd(n_prompt, n_sample)), 256))
    rows_p = (depth, batch, seq, N_HEADS, HEAD_DIM)
    rows_s = (depth, dec_batch, dec_seq, N_HEADS, HEAD_DIM)
    return (y_prompt.reshape(batch, seq, d), y_sample.reshape(dec_batch, dec_seq, d),
            kp.reshape(rows_p), vp.reshape(rows_p), jnp.stack(sp), ks.reshape(rows_s), vs.reshape(rows_s), jnp.stack(ss))
```

```python
import functools

import jax
import jax.numpy as jnp
import numpy as np
from jax import lax
from jax.experimental import pallas as pl
from jax.experimental.pallas import tpu as pltpu

F32 = jnp.float32
BF16 = jnp.bfloat16

HEAD_DIM = 128
N_HEADS = 8
WIDTH = N_HEADS * HEAD_DIM
PAGE = 128
TOP_K = 2
EPS = 1e-6
ROPE_BASE = 10000.0
NEG_INF = float("-inf")

SEC_SQ, SEC_SK, SEC_SV, SEC_RQ, SEC_RK, SEC_RV, SEC_RG = (s * N_HEADS for s in range(7))
COL_GA = 7 * WIDTH
D_IN_SECTIONS = 7

VMEM_LIMIT = 56 * 1024 * 1024


def _cparams(*semantics):
    return pltpu.CompilerParams(dimension_semantics=semantics, vmem_limit_bytes=VMEM_LIMIT)


def _dot(a, b):
    return jnp.dot(a, b, preferred_element_type=F32)


def _dot_nt(a, b):
    return lax.dot_general(a, b, (((1,), (1,)), ((), ())), preferred_element_type=F32)


def _sigmoid(x):
    return 1.0 / (1.0 + jnp.exp(-x))


def _rms_rows(x):
    return x * lax.rsqrt(jnp.mean(x * x, axis=-1, keepdims=True) + EPS)


def _inproj_kernel(x_ref, g_ref, w_ref, rope_ref, qg_ref, kg_ref, o_ref, h_ref, *, tn):
    j = pl.program_id(1)

    @pl.when(j == 0)
    def _():
        h_ref[...] = (_rms_rows(x_ref[...]) * g_ref[...]).astype(BF16)

    acc = _dot(h_ref[...], w_ref[...])
    sec = j // (WIDTH // tn)
    heads = tn // HEAD_DIM

    @pl.when(sec <= 1)
    def _():
        gain = jnp.where(sec == 0, qg_ref[...], kg_ref[...])
        for hh in range(heads):
            sl = slice(hh * HEAD_DIM, (hh + 1) * HEAD_DIM)
            o_ref[:, sl] = _rms_rows(acc[:, sl]) * gain

    @pl.when(jnp.logical_or(sec == 3, sec == 4))
    def _():
        cos = rope_ref[:, :HEAD_DIM]
        sin = rope_ref[:, HEAD_DIM:]
        scale = jnp.where(sec == 4, HEAD_DIM ** -0.5, 1.0).astype(F32)
        for hh in range(heads):
            sl = slice(hh * HEAD_DIM, (hh + 1) * HEAD_DIM)
            blk = acc[:, sl]
            o_ref[:, sl] = (blk * cos + pltpu.roll(blk, HEAD_DIM // 2, 1) * sin) * scale

    @pl.when(jnp.logical_and(sec != 0, jnp.logical_and(sec != 1, jnp.logical_and(sec != 3, sec != 4))))
    def _():
        o_ref[...] = acc


def _rope_table(seq, dec_seq, past_len, tm):
    half = HEAD_DIM // 2
    inv_freq = ROPE_BASE ** (-jnp.arange(half, dtype=F32) / half)
    pos = jnp.concatenate([jnp.arange(seq), past_len + (jnp.arange(tm) % dec_seq)]).astype(F32)
    ang = pos[:, None] * inv_freq[None, :]
    cos, sin = jnp.cos(ang), jnp.sin(ang)
    return jnp.concatenate([cos, cos, -sin, sin], axis=-1)


def _in_projection(x, g, w, rope_tab, qg, kg, *, n_prompt, seq, tm, tn):
    t, d = x.shape
    d_in = w.shape[1]
    assert t % tm == 0 and d_in % tn == 0 and WIDTH % tn == 0 and seq % tm == 0 and n_prompt % tm == 0
    n_prompt_tiles = n_prompt // tm
    per_seq = seq // tm

    def rope_map(i, j):
        return (jnp.where(i < n_prompt_tiles, i % per_seq, per_seq), 0)

    return pl.pallas_call(
        functools.partial(_inproj_kernel, tn=tn),
        grid=(t // tm, d_in // tn),
        in_specs=[
            pl.BlockSpec((tm, d), lambda i, j: (i, 0)),
            pl.BlockSpec((1, d), lambda i, j: (0, 0)),
            pl.BlockSpec((d, tn), lambda i, j: (0, j)),
            pl.BlockSpec((tm, 2 * HEAD_DIM), rope_map),
            pl.BlockSpec((1, HEAD_DIM), lambda i, j: (0, 0)),
            pl.BlockSpec((1, HEAD_DIM), lambda i, j: (0, 0)),
        ],
        out_specs=pl.BlockSpec((tm, tn), lambda i, j: (i, j)),
        out_shape=jax.ShapeDtypeStruct((t, d_in), F32),
        scratch_shapes=[pltpu.VMEM((tm, d), BF16)],
        compiler_params=_cparams("parallel", "arbitrary"),
        name="in_projection",
    )(x, g.reshape(1, d), w, rope_tab, qg.reshape(1, HEAD_DIM), kg.reshape(1, HEAD_DIM))


LOG2E = 1.4426950408889634
SB_SCALE2 = HEAD_DIM ** -0.5 * LOG2E


def _sb_block(z2, valid, tri):
    soft = jnp.log2(1.0 + jnp.exp2(-jnp.abs(z2)))
    log_beta = jnp.minimum(z2, 0.0) - soft
    log_1m = -jnp.maximum(z2, 0.0) - soft
    if valid is not None:
        log_1m = jnp.where(valid, log_1m, 0.0)
    hi = log_1m.astype(BF16)
    lo = (log_1m - hi.astype(F32)).astype(BF16)
    logw = log_beta + (_dot(hi, tri) + _dot(lo, tri))
    if valid is not None:
        logw = jnp.where(valid, logw, NEG_INF)
    return logw, jnp.sum(log_1m, axis=-1, keepdims=True)


def _tri(n):
    idx = jnp.arange(n)
    return (idx[:, None] > idx[None, :]).astype(BF16)


def _sb_prompt_kernel(bias_ref, q_ref, k_ref, v_ref, tri_ref, o_ref, kb_ref, vb_ref, *, tq, hg):
    h0 = pl.program_id(1) * hg
    qi = pl.program_id(2)

    @pl.when(qi == 0)
    def _():
        kb_ref[...] = k_ref[...].astype(BF16)
        vb_ref[...] = v_ref[...].astype(BF16)

    lanes = [slice(j * HEAD_DIM, (j + 1) * HEAD_DIM) for j in range(hg)]
    q = [q_ref[:, sl].astype(BF16) for sl in lanes]
    bias2 = [bias_ref[h0 + j] * LOG2E for j in range(hg)]
    tri = tri_ref[...]
    row = lax.broadcasted_iota(jnp.int32, (tq, tq), 0)
    col = lax.broadcasted_iota(jnp.int32, (tq, tq), 1)

    def keys(ref, kb, j):
        return ref[pl.ds(pl.multiple_of(kb * tq, tq), tq), lanes[j]]

    def scores(kb, valid):
        return tuple(_sb_block(_dot_nt(q[j], keys(kb_ref, kb, j)) * SB_SCALE2 + bias2[j], valid, tri)
                     for j in range(hg))

    def attend(kb, blk, state):
        new = []
        for j in range(hg):
            (logw, rowsum), (carry, acc) = blk[j], state[j]
            a = jnp.exp2(logw + carry).astype(BF16)
            new.append((carry + rowsum, acc + _dot(a, keys(vb_ref, kb, j))))
        return tuple(new)

    def body(it, carried):
        blk, state = carried
        return scores(qi - 1 - it, None), attend(qi - it, blk, state)

    state = tuple((jnp.zeros((tq, 1), F32), jnp.zeros((tq, HEAD_DIM), F32)) for _ in range(hg))
    blk = scores(qi, col < row)
    blk, state = lax.fori_loop(0, qi, body, (blk, state))
    state = attend(0, blk, state)
    for j in range(hg):
        o_ref[:, lanes[j]] = state[j][1].astype(o_ref.dtype)


def _sb_prompt(p, bias, *, batch, seq, tq, hg):
    nq = seq // tq
    wide = hg * HEAD_DIM
    return pl.pallas_call(
        functools.partial(_sb_prompt_kernel, tq=tq, hg=hg),
        grid_spec=pltpu.PrefetchScalarGridSpec(
            num_scalar_prefetch=0,
            grid=(batch, N_HEADS // hg, nq),
            in_specs=[
                pl.BlockSpec(memory_space=pltpu.SMEM),
                pl.BlockSpec((tq, wide), lambda b, h, qi: (b * nq + qi, SEC_SQ // hg + h)),
                pl.BlockSpec((seq, wide), lambda b, h, qi: (b, SEC_SK // hg + h)),
                pl.BlockSpec((seq, wide), lambda b, h, qi: (b, SEC_SV // hg + h)),
                pl.BlockSpec((tq, tq), lambda b, h, qi: (0, 0)),
            ],
            out_specs=pl.BlockSpec((tq, wide), lambda b, h, qi: (b * nq + qi, h)),
            scratch_shapes=[pltpu.VMEM((seq, wide), BF16), pltpu.VMEM((seq, wide), BF16)],
        ),
        out_shape=jax.ShapeDtypeStruct((batch * seq, WIDTH), BF16),
        compiler_params=_cparams("parallel", "parallel", "arbitrary"),
        name="sb_prompt",
    )(bias, p, p, p, _tri(tq))


def _sb_decode_kernel(pt_ref, q_ref, kn_ref, vn_ref, *rest, n_group, dec_seq):
    k_refs = rest[:n_group]
    v_refs = rest[n_group:2 * n_group]
    bias_ref, tri_ref, o_ref, kpad_ref, vpad_ref, carry_ref, acc_ref = rest[2 * n_group:]
    g = pl.program_id(1)
    rows = N_HEADS * dec_seq
    tri = tri_ref[...]
    bias2 = bias_ref[...] * LOG2E
    qs = [q_ref[:, h * HEAD_DIM:(h + 1) * HEAD_DIM].astype(BF16) for h in range(N_HEADS)]

    def head_rows(ref, h):
        return ref[pl.ds(h, PAGE, stride=N_HEADS), :].astype(BF16)

    def process(pages, valid):
        n = len(pages)
        z = jnp.concatenate([_dot_nt(qs[h], head_rows(k_ref, h)) for k_ref, _ in pages for h in range(N_HEADS)],
                            axis=0)
        logw, rowsum = _sb_block(z * SB_SCALE2 + jnp.concatenate([bias2] * n, axis=0), valid, tri)
        carry = carry_ref[...]
        carries = []
        for r in range(n):
            carries.append(carry)
            carry = carry + rowsum[r * rows:(r + 1) * rows]
        carry_ref[...] = carry
        a = jnp.exp2(logw + jnp.concatenate(carries, axis=0))
        for h in range(N_HEADS):
            sl = slice(h * HEAD_DIM, (h + 1) * HEAD_DIM)
            acc = acc_ref[:, sl]
            for r, (_, v_ref) in enumerate(pages):
                ah = a[r * rows + h * dec_seq:r * rows + (h + 1) * dec_seq, :].astype(BF16)
                acc = acc + _dot(ah, head_rows(v_ref, h))
            acc_ref[:, sl] = acc

    @pl.when(g == 0)
    def _():
        carry_ref[...] = jnp.zeros_like(carry_ref)
        acc_ref[...] = jnp.zeros_like(acc_ref)
        kpad_ref[...] = jnp.zeros_like(kpad_ref)
        vpad_ref[...] = jnp.zeros_like(vpad_ref)
        kpad_ref[:rows] = kn_ref[...]
        vpad_ref[:rows] = vn_ref[...]
        t_idx = lax.broadcasted_iota(jnp.int32, (rows, PAGE), 0) % dec_seq
        s_idx = lax.broadcasted_iota(jnp.int32, (rows, PAGE), 1)
        process([(kpad_ref, vpad_ref)], s_idx < t_idx)

    process([(k_refs[r], v_refs[r]) for r in range(n_group - 1, -1, -1)], None)

    @pl.when(g == pl.num_programs(1) - 1)
    def _():
        o_ref[...] = acc_ref[...]


def _sb_decode(p, k_new, v_new, cache_k, cache_v, page_table, bias, *, layer, n_prompt, n_group):
    dec_batch, dec_seq = k_new.shape[:2]
    n_pages = page_table.shape[1]
    assert n_pages % n_group == 0 and dec_seq == 8
    n_steps = n_pages // n_group
    rows = N_HEADS * dec_seq
    q_row0 = n_prompt // dec_seq
    page_rows = PAGE * N_HEADS
    cache_k = cache_k.reshape(cache_k.shape[:2] + (page_rows, HEAD_DIM))
    cache_v = cache_v.reshape(cache_v.shape[:2] + (page_rows, HEAD_DIM))
    k_new = k_new.reshape(dec_batch, rows, HEAD_DIM)
    v_new = v_new.reshape(dec_batch, rows, HEAD_DIM)

    def page_map(r):
        def index(b, g, pt):
            return (layer, pt[b * n_pages + n_pages - (g + 1) * n_group + r], 0, 0)
        return index

    page_specs = [pl.BlockSpec((None, None, page_rows, HEAD_DIM), page_map(r)) for r in range(n_group)]
    new_spec = pl.BlockSpec((None, rows, HEAD_DIM), lambda b, g, pt: (b, 0, 0))
    bias_rows = jnp.broadcast_to(jnp.repeat(bias.astype(F32), dec_seq)[:, None], (rows, PAGE))
    return pl.pallas_call(
        functools.partial(_sb_decode_kernel, n_group=n_group, dec_seq=dec_seq),
        grid_spec=pltpu.PrefetchScalarGridSpec(
            num_scalar_prefetch=1,
            grid=(dec_batch, n_steps),
            in_specs=[pl.BlockSpec((dec_seq, WIDTH), lambda b, g, pt: (q_row0 + b, SEC_SQ // N_HEADS)),
                      new_spec, new_spec] + page_specs + page_specs + [
                pl.BlockSpec((rows, PAGE), lambda b, g, pt: (0, 0)),
                pl.BlockSpec((PAGE, PAGE), lambda b, g, pt: (0, 0)),
            ],
            out_specs=pl.BlockSpec((dec_seq, WIDTH), lambda b, g, pt: (b, 0)),
            scratch_shapes=[
                pltpu.VMEM((page_rows, HEAD_DIM), F32),
                pltpu.VMEM((page_rows, HEAD_DIM), F32),
                pltpu.VMEM((rows, 1), F32),
                pltpu.VMEM((dec_seq, WIDTH), F32),
            ],
        ),
        out_shape=jax.ShapeDtypeStruct((dec_batch * dec_seq, WIDTH), F32),
        compiler_params=_cparams("parallel", "arbitrary"),
        name="sb_decode",
    )(page_table.reshape(-1), p, k_new, v_new, *([cache_k] * n_group), *([cache_v] * n_group),
      bias_rows, _tri(PAGE))


def _ret_consts(log_gamma, chunk):
    idx = jnp.arange(chunk, dtype=F32)
    diff = idx[:, None] - idx[None, :]
    causal = diff >= 0
    decay = jnp.where(causal[None], jnp.exp(jnp.where(causal, diff, 0.0)[None] * log_gamma[:, None, None]), 0.0)
    q_decay = jnp.exp((idx + 1.0)[None, :] * log_gamma[:, None])
    k_decay = jnp.exp((chunk - 1.0 - idx)[None, :] * log_gamma[:, None])
    lanes = (N_HEADS, chunk, HEAD_DIM)
    return (decay, jnp.broadcast_to(q_decay[:, :, None], lanes), jnp.broadcast_to(k_decay[:, :, None], lanes),
            jnp.exp(chunk * log_gamma))


def _ret_chunk(q, k, v, rg, state, decay, qd, kd, state_decay):
    qb, kb, vb = q.astype(BF16), k.astype(BF16), v.astype(BF16)
    scores = _dot_nt(qb, kb) * decay
    o = _dot(scores.astype(BF16), vb) + _dot(qb, state.astype(BF16)) * qd
    new_state = state_decay * state + lax.dot_general((k * kd).astype(BF16), vb, (((0,), (0,)), ((), ())),
                                                      preferred_element_type=F32)
    out = _rms_rows(o) * (rg * _sigmoid(rg))
    return out, new_state


def _ret_prompt_kernel(sd_ref, q_ref, k_ref, v_ref, rg_ref, decay_ref, qd_ref, kd_ref, o_ref, s_ref, *,
                       chunk, n_chunks, hg):
    h0 = pl.program_id(1) * hg
    s_ref[...] = jnp.zeros_like(s_ref)

    def body(c, _):
        rows = pl.ds(pl.multiple_of(c * chunk, chunk), chunk)
        for j in range(hg):
            sl = slice(j * HEAD_DIM, (j + 1) * HEAD_DIM)
            out, new_state = _ret_chunk(q_ref[rows, sl], k_ref[rows, sl], v_ref[rows, sl], rg_ref[rows, sl],
                                        s_ref[j], decay_ref[j], qd_ref[j], kd_ref[j], sd_ref[h0 + j])
            s_ref[j] = new_state
            o_ref[rows, sl] = out.astype(o_ref.dtype)
        return 0

    lax.fori_loop(0, n_chunks, body, 0)


def _ret_prompt(p, log_gamma, *, batch, seq, chunk, hg):
    decay, qd, kd, sd = _ret_consts(log_gamma, chunk)
    wide = hg * HEAD_DIM
    col = lambda sec: pl.BlockSpec((seq, wide), lambda b, h: (b, sec // hg + h))
    per_head = lambda n: pl.BlockSpec((hg, chunk, n), lambda b, h: (h, 0, 0))
    return pl.pallas_call(
        functools.partial(_ret_prompt_kernel, chunk=chunk, n_chunks=seq // chunk, hg=hg),
        grid=(batch, N_HEADS // hg),
        in_specs=[pl.BlockSpec(memory_space=pltpu.SMEM), col(SEC_RQ), col(SEC_RK), col(SEC_RV), col(SEC_RG),
                  per_head(chunk), per_head(HEAD_DIM), per_head(HEAD_DIM)],
        out_specs=[pl.BlockSpec((seq, wide), lambda b, h: (b, h)),
                   pl.BlockSpec((None, hg, HEAD_DIM, HEAD_DIM), lambda b, h: (b, h, 0, 0))],
        out_shape=[jax.ShapeDtypeStruct((batch * seq, WIDTH), BF16),
                   jax.ShapeDtypeStruct((batch, N_HEADS, HEAD_DIM, HEAD_DIM), F32)],
        compiler_params=_cparams("parallel", "parallel"),
        name="retention_prompt",
    )(sd, p, p, p, p, decay, qd, kd)


def _ret_sample_kernel(sd_ref, q_ref, k_ref, v_ref, rg_ref, s_in_ref, decay_ref, qd_ref, kd_ref, o_ref, s_out_ref):
    for h in range(N_HEADS):
        sl = slice(h * HEAD_DIM, (h + 1) * HEAD_DIM)
        out, new_state = _ret_chunk(q_ref[:, sl], k_ref[:, sl], v_ref[:, sl], rg_ref[:, sl], s_in_ref[h],
                                    decay_ref[h], qd_ref[h], kd_ref[h], sd_ref[h])
        o_ref[:, sl] = out
        s_out_ref[h] = new_state


def _ret_sample(p, state, log_gamma, *, n_prompt, dec_seq):
    dec_batch = state.shape[0]
    decay, qd, kd, sd = _ret_consts(log_gamma, dec_seq)
    row0 = n_prompt // dec_seq
    col = lambda sec: pl.BlockSpec((dec_seq, WIDTH), lambda b: (row0 + b, sec // N_HEADS))
    full = lambda a: pl.BlockSpec(a.shape, lambda b: (0,) * a.ndim)
    state_spec = pl.BlockSpec((None, N_HEADS, HEAD_DIM, HEAD_DIM), lambda b: (b, 0, 0, 0))
    return pl.pallas_call(
        _ret_sample_kernel,
        grid=(dec_batch,),
        in_specs=[pl.BlockSpec(memory_space=pltpu.SMEM), col(SEC_RQ), col(SEC_RK), col(SEC_RV), col(SEC_RG),
                  state_spec, full(decay), full(qd), full(kd)],
        out_specs=[pl.BlockSpec((dec_seq, WIDTH), lambda b: (b, 0)), state_spec],
        out_shape=[jax.ShapeDtypeStruct((dec_batch * dec_seq, WIDTH), F32),
                   jax.ShapeDtypeStruct(state.shape, F32)],
        compiler_params=_cparams("parallel"),
        name="retention_sample",
    )(sd, p, p, p, p, state, decay, qd, kd)


def _merge_kernel(a_ref, b_ref, wa_ref, wb_ref, ga_ref, gb_ref, o_ref):
    ya = _dot(a_ref[...], wa_ref[...])
    yb = _dot(b_ref[...], wb_ref[...])
    o_ref[...] = (_sigmoid(ga_ref[...]) * ya + _sigmoid(gb_ref[...]) * yb).astype(o_ref.dtype)


def _merge(o_sb, ret, w_pa, w_pb, p, *, tm, tn):
    t = o_sb.shape[0]
    d = w_pa.shape[1]
    ga0 = COL_GA // tn
    gb0 = (COL_GA + d) // tn
    return pl.pallas_call(
        _merge_kernel,
        grid=(t // tm, d // tn),
        in_specs=[
            pl.BlockSpec((tm, WIDTH), lambda i, j: (i, 0)),
            pl.BlockSpec((tm, WIDTH), lambda i, j: (i, 0)),
            pl.BlockSpec((WIDTH, tn), lambda i, j: (0, j)),
            pl.BlockSpec((WIDTH, tn), lambda i, j: (0, j)),
            pl.BlockSpec((tm, tn), lambda i, j: (i, ga0 + j)),
            pl.BlockSpec((tm, tn), lambda i, j: (i, gb0 + j)),
        ],
        out_specs=pl.BlockSpec((tm, tn), lambda i, j: (i, j)),
        out_shape=jax.ShapeDtypeStruct((t, d), BF16),
        compiler_params=_cparams("parallel", "arbitrary"),
        name="branch_merge",
    )(o_sb, ret, w_pa, w_pb, p, p)


def _out_proj_kernel(m_ref, w_ref, x_ref, o_ref):
    o_ref[...] = x_ref[...] + _dot(m_ref[...], w_ref[...])


def _out_proj(m, w_o, x, *, tm, tn):
    t, d = x.shape
    return pl.pallas_call(
        _out_proj_kernel,
        grid=(t // tm, d // tn),
        in_specs=[
            pl.BlockSpec((tm, d), lambda i, j: (i, 0)),
            pl.BlockSpec((d, tn), lambda i, j: (0, j)),
            pl.BlockSpec((tm, tn), lambda i, j: (i, j)),
        ],
        out_specs=pl.BlockSpec((tm, tn), lambda i, j: (i, j)),
        out_shape=jax.ShapeDtypeStruct((t, d), F32),
        compiler_params=_cparams("parallel", "arbitrary"),
        name="out_projection",
    )(m, w_o, x)


def _swiglu(h, wg, wu, wd):
    gate = _dot(h, wg)
    up = _dot(h, wu)
    return _dot((gate * _sigmoid(gate) * up).astype(BF16), wd)


def _dense_ffn_kernel(x_ref, g_ref, wg_ref, wu_ref, wd_ref, o_ref, h_ref):
    @pl.when(pl.program_id(1) == 0)
    def _():
        x = x_ref[...]
        h_ref[...] = (_rms_rows(x) * g_ref[...]).astype(BF16)
        o_ref[...] = x

    o_ref[...] += _swiglu(h_ref[...], wg_ref[...], wu_ref[...], wd_ref[...])


def _dense_ffn(x, g, wg, wu, wd, *, tm, tf):
    t, d = x.shape
    d_ff = wg.shape[1]
    return pl.pallas_call(
        _dense_ffn_kernel,
        grid=(t // tm, d_ff // tf),
        in_specs=[
            pl.BlockSpec((tm, d), lambda i, f: (i, 0), pipeline_mode=pl.Buffered(1)),
            pl.BlockSpec((1, d), lambda i, f: (0, 0)),
            pl.BlockSpec((d, tf), lambda i, f: (0, f)),
            pl.BlockSpec((d, tf), lambda i, f: (0, f)),
            pl.BlockSpec((tf, d), lambda i, f: (f, 0)),
        ],
        out_specs=pl.BlockSpec((tm, d), lambda i, f: (i, 0)),
        out_shape=jax.ShapeDtypeStruct((t, d), F32),
        scratch_shapes=[pltpu.VMEM((tm, d), BF16)],
        compiler_params=_cparams("parallel", "arbitrary"),
        name="dense_ffn",
    )(x, g.reshape(1, d), wg, wu, wd)


def _expert_ffn_kernel(te_ref, tr_ref, h_ref, wg_ref, wu_ref, wd_ref, o_ref, wgb_ref, wub_ref, wdb_ref, *, sub):
    i = pl.program_id(0)
    n_sub = (tr_ref[i] + (sub - 1)) // sub

    @pl.when(pl.program_id(1) == 0)
    def _():
        o_ref[...] = jnp.zeros_like(o_ref)

    @pl.when(n_sub > 0)
    def _():
        wgb_ref[...] = wg_ref[...].astype(BF16)
        wub_ref[...] = wu_ref[...].astype(BF16)
        wdb_ref[...] = wd_ref[...].astype(BF16)

        def run(first, count):
            rs = pl.ds(pl.multiple_of(first * sub, sub), count * sub)
            o_ref[rs, :] += _swiglu(h_ref[rs, :], wgb_ref[...], wub_ref[...], wdb_ref[...])

        def pair(s, _):
            run(2 * s, 2)
            return 0

        lax.fori_loop(0, n_sub // 2, pair, 0)

        @pl.when(n_sub % 2 == 1)
        def _():
            run(n_sub - 1, 1)


def _expert_ffn(hs, wg, wu, wd, tile_expert, tile_rows, *, tm, tf, sub):
    r, d = hs.shape
    d_ff = wg.shape[2]
    nf = d_ff // tf

    def f_idx(i, f, tr):
        return jnp.where(tr[i] > 0, f, nf - 1)

    return pl.pallas_call(
        functools.partial(_expert_ffn_kernel, sub=sub),
        grid_spec=pltpu.PrefetchScalarGridSpec(
            num_scalar_prefetch=2,
            grid=(r // tm, nf),
            in_specs=[
                pl.BlockSpec((tm, d), lambda i, f, te, tr: (i, 0), pipeline_mode=pl.Buffered(1)),
                pl.BlockSpec((None, d, tf), lambda i, f, te, tr: (te[i], 0, f_idx(i, f, tr))),
                pl.BlockSpec((None, d, tf), lambda i, f, te, tr: (te[i], 0, f_idx(i, f, tr))),
                pl.BlockSpec((None, tf, d), lambda i, f, te, tr: (te[i], f_idx(i, f, tr), 0)),
            ],
            out_specs=pl.BlockSpec((tm, d), lambda i, f, te, tr: (i, 0)),
            scratch_shapes=[pltpu.VMEM((d, tf), BF16), pltpu.VMEM((d, tf), BF16), pltpu.VMEM((tf, d), BF16)],
        ),
        out_shape=jax.ShapeDtypeStruct((r, d), F32),
        compiler_params=_cparams("parallel", "arbitrary"),
        name="expert_ffn",
    )(tile_expert, tile_rows, hs, wg, wu, wd)


def _router_kernel(x_ref, g_ref, w_ref, o_ref, hn_ref, *, n_experts):
    h = _rms_rows(x_ref[...]) * g_ref[...]
    tm, d = h.shape
    n_col = d // HEAD_DIM
    for c in range(n_col):
        hn_ref[pl.ds(c, tm, stride=n_col), :] = h[:, c * HEAD_DIM:(c + 1) * HEAD_DIM]
    w = w_ref[...]
    h_hi = h.astype(BF16)
    h_lo = (h - h_hi.astype(F32)).astype(BF16)
    w_hi = w.astype(BF16)
    w_lo = (w - w_hi.astype(F32)).astype(BF16)
    logits = _dot(h_hi, w_hi) + (_dot(h_hi, w_lo) + _dot(h_lo, w_hi))
    lane = lax.broadcasted_iota(jnp.int32, logits.shape, 1).astype(F32)
    lg = jnp.where(lane < n_experts, logits, NEG_INF)
    m1 = jnp.max(lg, axis=-1, keepdims=True)
    i1 = jnp.min(jnp.where(lg == m1, lane, float(HEAD_DIM)), axis=-1, keepdims=True)
    lg2 = jnp.where(lane == i1, NEG_INF, lg)
    m2 = jnp.max(lg2, axis=-1, keepdims=True)
    i2 = jnp.min(jnp.where(lg2 == m2, lane, float(HEAD_DIM)), axis=-1, keepdims=True)
    e = jnp.exp(m2 - m1)
    g1 = 1.0 / (1.0 + e)
    g2 = e / (1.0 + e)
    o_ref[...] = jnp.where(lane == 0, i1, jnp.where(lane == 1, i2, jnp.where(lane == 2, g1, jnp.where(lane == 3, g2, 0.0))))


def _router(x, g, w_router, *, tm):
    t, d = x.shape
    n_experts = w_router.shape[1]
    w_pad = jnp.pad(w_router, ((0, 0), (0, HEAD_DIM - n_experts)))
    return pl.pallas_call(
        functools.partial(_router_kernel, n_experts=n_experts),
        grid=(t // tm,),
        in_specs=[
            pl.BlockSpec((tm, d), lambda i: (i, 0)),
            pl.BlockSpec((1, d), lambda i: (0, 0)),
            pl.BlockSpec((d, HEAD_DIM), lambda i: (0, 0)),
        ],
        out_specs=[pl.BlockSpec((tm, HEAD_DIM), lambda i: (i, 0)),
                   pl.BlockSpec((tm * (d // HEAD_DIM), HEAD_DIM), lambda i: (i, 0))],
        out_shape=[jax.ShapeDtypeStruct((t, HEAD_DIM), F32),
                   jax.ShapeDtypeStruct((t * (d // HEAD_DIM), HEAD_DIM), F32)],
        compiler_params=_cparams("parallel"),
        name="router",
    )(x, g.reshape(1, d), w_pad)


def _row_copy(src_hbm, dst_ref, sem, src_row, dst_row):
    return pltpu.make_async_copy(src_hbm.at[pl.ds(src_row, 1)], dst_ref.at[pl.ds(dst_row, 1)], sem)


def _slab_copy(src_hbm, dst_ref, sem, src_token, dst_token, n_col):
    return pltpu.make_async_copy(src_hbm.at[pl.ds(pl.multiple_of(src_token * n_col, n_col), n_col)],
                                 dst_ref.at[pl.ds(pl.multiple_of(dst_token * n_col, n_col), n_col)], sem)


def _gather_rows_kernel(idx_ref, idx_next_ref, hn_hbm, o_ref, buf_ref, sem, *, rows, n_col):
    i = pl.program_id(0)
    slot = i % 2

    def start_all(ids_ref, s):
        def start(r, _):
            _slab_copy(hn_hbm, buf_ref.at[s], sem.at[s], ids_ref[0, r], r, n_col).start()
            return 0
        lax.fori_loop(0, rows, start, 0, unroll=8)

    @pl.when(i == 0)
    def _():
        start_all(idx_ref, slot)

    @pl.when(i + 1 < pl.num_programs(0))
    def _():
        start_all(idx_next_ref, 1 - slot)

    def wait(r, _):
        _slab_copy(hn_hbm, buf_ref.at[slot], sem.at[slot], 0, r, n_col).wait()
        return 0

    lax.fori_loop(0, rows, wait, 0, unroll=8)
    for c in range(n_col):
        o_ref[:, c * HEAD_DIM:(c + 1) * HEAD_DIM] = buf_ref[slot, pl.ds(c, rows, stride=n_col), :].astype(BF16)


def _gather_rows(hn, row_token, *, rows, d):
    r = row_token.shape[0]
    n_col = d // HEAD_DIM
    steps = r // rows
    ids = row_token.reshape(steps, 1, rows)
    return pl.pallas_call(
        functools.partial(_gather_rows_kernel, rows=rows, n_col=n_col),
        grid=(steps,),
        in_specs=[
            pl.BlockSpec((None, 1, rows), lambda i: (i, 0, 0), memory_space=pltpu.SMEM),
            pl.BlockSpec((None, 1, rows), lambda i: (jnp.minimum(i + 1, steps - 1), 0, 0), memory_space=pltpu.SMEM),
            pl.BlockSpec(memory_space=pl.ANY),
        ],
        out_specs=pl.BlockSpec((rows, d), lambda i: (i, 0)),
        out_shape=jax.ShapeDtypeStruct((r, d), BF16),
        scratch_shapes=[pltpu.VMEM((2, rows * n_col, HEAD_DIM), F32), pltpu.SemaphoreType.DMA((2,))],
        compiler_params=_cparams("arbitrary"),
        name="gather_expert_rows",
    )(ids, ids, hn)


def _combine_kernel(pos_ref, pos_next_ref, x_ref, route_ref, y_hbm, op_ref, os_ref, ya_ref, yb_ref, sem, *,
                    rows, prompt_steps):
    i = pl.program_id(0)
    slot = i % 2

    def start_all(p_ref, s):
        def start(r, _):
            _row_copy(y_hbm, ya_ref.at[s], sem.at[s], p_ref[0, r], r).start()
            _row_copy(y_hbm, yb_ref.at[s], sem.at[s], p_ref[0, rows + r], r).start()
            return 0
        lax.fori_loop(0, rows, start, 0, unroll=8)

    @pl.when(i == 0)
    def _():
        start_all(pos_ref, slot)

    @pl.when(i + 1 < pl.num_programs(0))
    def _():
        start_all(pos_next_ref, 1 - slot)

    def wait(r, _):
        _row_copy(y_hbm, ya_ref.at[slot], sem.at[slot], 0, r).wait()
        _row_copy(y_hbm, yb_ref.at[slot], sem.at[slot], 0, r).wait()
        return 0

    lax.fori_loop(0, rows, wait, 0, unroll=8)
    route = route_ref[...]
    out = x_ref[...] + route[:, 2:3] * ya_ref[slot] + route[:, 3:4] * yb_ref[slot]

    @pl.when(i < prompt_steps)
    def _():
        op_ref[...] = out

    @pl.when(i >= prompt_steps)
    def _():
        os_ref[...] = out


def _combine(x, route, y_sorted, pos, *, rows, n_prompt):
    t, d = x.shape
    assert n_prompt % rows == 0 and 0 < n_prompt < t
    prompt_steps = n_prompt // rows
    steps = t // rows
    return pl.pallas_call(
        functools.partial(_combine_kernel, rows=rows, prompt_steps=prompt_steps),
        grid=(steps,),
        in_specs=[
            pl.BlockSpec((None, 1, TOP_K * rows), lambda i: (i, 0, 0), memory_space=pltpu.SMEM),
            pl.BlockSpec((None, 1, TOP_K * rows), lambda i: (jnp.minimum(i + 1, steps - 1), 0, 0),
                         memory_space=pltpu.SMEM),
            pl.BlockSpec((rows, d), lambda i: (i, 0)),
            pl.BlockSpec((rows, HEAD_DIM), lambda i: (i, 0)),
            pl.BlockSpec(memory_space=pl.ANY),
        ],
        out_specs=[pl.BlockSpec((rows, d), lambda i: (jnp.minimum(i, prompt_steps - 1), 0)),
                   pl.BlockSpec((rows, d), lambda i: (jnp.maximum(i - prompt_steps, 0), 0))],
        out_shape=[jax.ShapeDtypeStruct((n_prompt, d), F32), jax.ShapeDtypeStruct((t - n_prompt, d), F32)],
        scratch_shapes=[pltpu.VMEM((2, rows, d), F32), pltpu.VMEM((2, rows, d), F32), pltpu.SemaphoreType.DMA((2,))],
        compiler_params=_cparams("arbitrary"),
        name="combine_expert_rows",
    )(pos, pos, x, route, y_sorted)


def _routing_tables(route, n_experts, *, tm, rows):
    t = route.shape[0]
    expert = route[:, :TOP_K].astype(jnp.int32).reshape(-1)
    onehot = (expert[:, None] == jnp.arange(n_experts)[None, :]).astype(jnp.int32)
    rank = jnp.sum((jnp.cumsum(onehot, axis=0) - onehot) * onehot, axis=1)
    count = jnp.sum(onehot, axis=0)
    tiles = (count + tm - 1) // tm
    tile_end = jnp.cumsum(tiles)
    start = (tile_end - tiles) * tm
    dest = start[expert] + rank
    n_rows = (t * TOP_K // tm + n_experts) * tm
    n_tiles = n_rows // tm
    row_token = jnp.zeros((n_rows,), jnp.int32).at[dest].set(jnp.arange(t * TOP_K, dtype=jnp.int32) // TOP_K)
    tile_id = jnp.arange(n_tiles)
    tile_valid = tile_id < tile_end[-1]
    tile_expert = jnp.minimum(jnp.sum(tile_id[:, None] >= tile_end[None, :], axis=1), n_experts - 1)
    last_expert = tile_expert[jnp.maximum(tile_end[-1] - 1, 0)]
    first_tile = (tile_end - tiles)[tile_expert]
    tile_rows = jnp.clip(count[tile_expert] - (tile_id - first_tile) * tm, 0, tm)
    tile_rows = jnp.where(tile_valid, tile_rows, 0).astype(jnp.int32)
    tile_expert = jnp.where(tile_valid, tile_expert, last_expert).astype(jnp.int32)
    dest = dest.reshape(t // rows, rows, TOP_K)
    pos = jnp.concatenate([dest[:, :, 0], dest[:, :, 1]], axis=1).reshape(t // rows, 1, TOP_K * rows)
    return row_token, tile_expert, tile_rows, pos


def _moe(x, g, w_router, wg, wu, wd, *, n_prompt, tm_route, tm, tf, sub, rows):
    n_experts = w_router.shape[1]
    route, hn = _router(x, g, w_router, tm=tm_route)
    row_token, tile_expert, tile_rows, pos = _routing_tables(route, n_experts, tm=tm, rows=rows)
    hs = _gather_rows(hn, row_token, rows=rows, d=x.shape[1])
    ys = _expert_ffn(hs, wg, wu, wd, tile_expert, tile_rows, tm=tm, tf=tf, sub=sub)
    return _combine(x, route, ys, pos, rows=rows, n_prompt=n_prompt)


def _kv_rows_kernel(*refs, depth, tr, prompt_steps):
    ins, (kp_ref, vp_ref, ks_ref, vs_ref) = refs[:2 * depth], refs[2 * depth:]
    l = pl.program_id(0)
    i = pl.program_id(1)

    def emit(src_ref, dst_ref):
        for h in range(N_HEADS):
            dst_ref[pl.ds(h, tr, stride=N_HEADS), :] = src_ref[:, h * HEAD_DIM:(h + 1) * HEAD_DIM]

    for layer in range(depth):
        @pl.when(jnp.logical_and(l == layer, i < prompt_steps))
        def _():
            emit(ins[2 * layer], kp_ref)
            emit(ins[2 * layer + 1], vp_ref)

        @pl.when(jnp.logical_and(l == layer, i >= prompt_steps))
        def _():
            emit(ins[2 * layer], ks_ref)
            emit(ins[2 * layer + 1], vs_ref)


def _kv_rows(projections, *, n_prompt, tr):
    depth = len(projections)
    t = projections[0].shape[0]
    assert n_prompt % tr == 0 and (t - n_prompt) % tr == 0
    steps = t // tr
    prompt_steps = n_prompt // tr

    def src(layer, sec):
        def index(l, i):
            return (jnp.where(l == layer, i, jnp.where(l < layer, 0, steps - 1)), sec // N_HEADS)
        return pl.BlockSpec((tr, WIDTH), index)

    prompt_spec = pl.BlockSpec((None, tr * N_HEADS, HEAD_DIM), lambda l, i: (l, jnp.minimum(i, prompt_steps - 1), 0))
    sample_spec = pl.BlockSpec((None, tr * N_HEADS, HEAD_DIM), lambda l, i: (l, jnp.maximum(i - prompt_steps, 0), 0))
    prompt_shape = jax.ShapeDtypeStruct((depth, n_prompt * N_HEADS, HEAD_DIM), F32)
    sample_shape = jax.ShapeDtypeStruct((depth, (t - n_prompt) * N_HEADS, HEAD_DIM), F32)
    return pl.pallas_call(
        functools.partial(_kv_rows_kernel, depth=depth, tr=tr, prompt_steps=prompt_steps),
        grid=(depth, steps),
        in_specs=[src(layer, sec) for layer in range(depth) for sec in (SEC_SK, SEC_SV)],
        out_specs=[prompt_spec, prompt_spec, sample_spec, sample_spec],
        out_shape=[prompt_shape, prompt_shape, sample_shape, sample_shape],
        compiler_params=_cparams("arbitrary", "arbitrary"),
        name="kv_rows",
    )(*[p for p in projections for _ in range(2)])


def _pick(n, want):
    t = min(n, want)
    while n % t:
        t -= 8
    return t


def kernel(x_prompt, x_sample, cache_sb_k, cache_sb_v, state_ret, page_table, norm_attn, w_in, qnorm_g, knorm_g,
           sb_bias, w_pa, w_pb, w_o, norm_ffn, w_ff_gate, w_ff_up, w_ff_down, w_router, w_exp_gate, w_exp_up,
           w_exp_down):
    batch, seq, d = x_prompt.shape
    dec_batch, dec_seq, _ = x_sample.shape
    depth = w_in.shape[0]
    n_pages = page_table.shape[1]
    past_len = n_pages * cache_sb_k.shape[2]
    n_prompt = batch * seq
    n_sample = dec_batch * dec_seq
    t = n_prompt + n_sample
    log_gamma = jnp.log1p(-jnp.exp2(-5.0 - jnp.arange(N_HEADS, dtype=F32)))

    tm = _pick(int(np.gcd(seq, n_sample)), 1024)
    tq = _pick(seq, 256)
    chunk = _pick(seq, 128)
    tm_moe = _pick(TOP_K * t, 1024)
    rope_tab = _rope_table(seq, dec_seq, past_len, tm)
    w_in, w_pa, w_pb, w_o, w_ff_gate, w_ff_up, w_ff_down = (
        w.astype(BF16) for w in (w_in, w_pa, w_pb, w_o, w_ff_gate, w_ff_up, w_ff_down))

    x = jnp.concatenate([x_prompt.reshape(n_prompt, d), x_sample.reshape(n_sample, d)], axis=0)
    projections, sp, ss = [], [], []
    y_prompt = y_sample = None
    for l in range(depth):
        p = _in_projection(x, norm_attn[l], w_in[l], rope_tab, qnorm_g[l], knorm_g[l],
                           n_prompt=n_prompt, seq=seq, tm=tm, tn=1024)
        projections.append(p)
        new_rows = lambda sec: p[n_prompt:, sec * HEAD_DIM:sec * HEAD_DIM + WIDTH].reshape(
            dec_batch, dec_seq, N_HEADS, HEAD_DIM)

        o_sb_p = _sb_prompt(p, sb_bias[l], batch=batch, seq=seq, tq=tq, hg=4)
        o_sb_s = _sb_decode(p, new_rows(SEC_SK), new_rows(SEC_SV), cache_sb_k, cache_sb_v, page_table, sb_bias[l],
                            layer=l, n_prompt=n_prompt, n_group=_pick(n_pages * 8, 64) // 8)
        ret_p, state_p = _ret_prompt(p, log_gamma, batch=batch, seq=seq, chunk=chunk, hg=4)
        ret_s, state_s = _ret_sample(p, state_ret[l], log_gamma, n_prompt=n_prompt, dec_seq=dec_seq)
        sp.append(state_p)
        ss.append(state_s)

        o_sb = jnp.concatenate([o_sb_p, o_sb_s.astype(BF16)], axis=0)
        ret = jnp.concatenate([ret_p, ret_s.astype(BF16)], axis=0)
        m = _merge(o_sb, ret, w_pa[l], w_pb[l], p, tm=tm, tn=1024)
        x = _out_proj(m, w_o[l], x, tm=tm, tn=1024)

        i = l // 2
        if l % 2 == 0:
            x = _dense_ffn(x, norm_ffn[l], w_ff_gate[i], w_ff_up[i], w_ff_down[i], tm=tm, tf=512)
        else:
            y_prompt, y_sample = _moe(x, norm_ffn[l], w_router[i], w_exp_gate[i], w_exp_up[i], w_exp_down[i],
                                      n_prompt=n_prompt, tm_route=_pick(t, 512), tm=tm_moe, tf=512,
                                      sub=_pick(tm_moe, 256), rows=_pick(int(np.gcd(tm_moe, n_sample)), 128))
            if l + 1 < depth:
                x = jnp.concatenate([y_prompt, y_sample], axis=0)
    if depth % 2:
        y_prompt, y_sample = x[:n_prompt], x[n_prompt:]

    kp, vp, ks, vs = _kv_rows(projections, n_prompt=n_prompt, tr=_pick(int(np.gcd(n_prompt, n_sample)), 256))
    rows_p = (depth, batch, seq, N_HEADS, HEAD_DIM)
    rows_s = (depth, dec_batch, dec_seq, N_HEADS, HEAD_DIM)
    return (y_prompt.reshape(batch, seq, d), y_sample.reshape(dec_batch, dec_seq, d),
            kp.reshape(rows_p), vp.reshape(rows_p), jnp.stack(sp), ks.reshape(rows_s), vs.reshape(rows_s), jnp.stack(ss))
```

```python
import functools

import jax
import jax.numpy as jnp
import numpy as np
from jax import lax
from jax.experimental import pallas as pl
from jax.experimental.pallas import tpu as pltpu

F32 = jnp.float32
BF16 = jnp.bfloat16

HEAD_DIM = 128
N_HEADS = 8
WIDTH = N_HEADS * HEAD_DIM
PAGE = 128
TOP_K = 2
EPS = 1e-6
ROPE_BASE = 10000.0
NEG_INF = float("-inf")

SEC_SQ, SEC_SK, SEC_SV, SEC_RQ, SEC_RK, SEC_RV, SEC_RG = (s * N_HEADS for s in range(7))
COL_GA = 7 * WIDTH
D_IN_SECTIONS = 7

VMEM_LIMIT = 56 * 1024 * 1024


def _cparams(*semantics):
    return pltpu.CompilerParams(dimension_semantics=semantics, vmem_limit_bytes=VMEM_LIMIT)


def _dot(a, b):
    return jnp.dot(a, b, preferred_element_type=F32)


def _dot_nt(a, b):
    return lax.dot_general(a, b, (((1,), (1,)), ((), ())), preferred_element_type=F32)


def _sigmoid(x):
    return 1.0 / (1.0 + jnp.exp(-x))


def _rms_rows(x):
    return x * lax.rsqrt(jnp.mean(x * x, axis=-1, keepdims=True) + EPS)


def _inproj_kernel(x_ref, g_ref, w_ref, rope_ref, qg_ref, kg_ref, o_ref, h_ref, *, tn):
    j = pl.program_id(1)

    @pl.when(j == 0)
    def _():
        h_ref[...] = (_rms_rows(x_ref[...]) * g_ref[...]).astype(BF16)

    acc = _dot(h_ref[...], w_ref[...])
    sec = j // (WIDTH // tn)
    heads = tn // HEAD_DIM

    @pl.when(sec <= 1)
    def _():
        gain = jnp.where(sec == 0, qg_ref[...], kg_ref[...])
        for hh in range(heads):
            sl = slice(hh * HEAD_DIM, (hh + 1) * HEAD_DIM)
            o_ref[:, sl] = _rms_rows(acc[:, sl]) * gain

    @pl.when(jnp.logical_or(sec == 3, sec == 4))
    def _():
        cos = rope_ref[:, :HEAD_DIM]
        sin = rope_ref[:, HEAD_DIM:]
        scale = jnp.where(sec == 4, HEAD_DIM ** -0.5, 1.0).astype(F32)
        for hh in range(heads):
            sl = slice(hh * HEAD_DIM, (hh + 1) * HEAD_DIM)
            blk = acc[:, sl]
            o_ref[:, sl] = (blk * cos + pltpu.roll(blk, HEAD_DIM // 2, 1) * sin) * scale

    @pl.when(jnp.logical_and(sec != 0, jnp.logical_and(sec != 1, jnp.logical_and(sec != 3, sec != 4))))
    def _():
        o_ref[...] = acc


def _rope_table(seq, dec_seq, past_len, tm):
    half = HEAD_DIM // 2
    inv_freq = ROPE_BASE ** (-jnp.arange(half, dtype=F32) / half)
    pos = jnp.concatenate([jnp.arange(seq), past_len + (jnp.arange(tm) % dec_seq)]).astype(F32)
    ang = pos[:, None] * inv_freq[None, :]
    cos, sin = jnp.cos(ang), jnp.sin(ang)
    return jnp.concatenate([cos, cos, -sin, sin], axis=-1)


def _in_projection(x, g, w, rope_tab, qg, kg, *, n_prompt, seq, tm, tn):
    t, d = x.shape
    d_in = w.shape[1]
    assert t % tm == 0 and d_in % tn == 0 and WIDTH % tn == 0 and seq % tm == 0 and n_prompt % tm == 0
    n_prompt_tiles = n_prompt // tm
    per_seq = seq // tm

    def rope_map(i, j):
        return (jnp.where(i < n_prompt_tiles, i % per_seq, per_seq), 0)

    return pl.pallas_call(
        functools.partial(_inproj_kernel, tn=tn),
        grid=(t // tm, d_in // tn),
        in_specs=[
            pl.BlockSpec((tm, d), lambda i, j: (i, 0)),
            pl.BlockSpec((1, d), lambda i, j: (0, 0)),
            pl.BlockSpec((d, tn), lambda i, j: (0, j)),
            pl.BlockSpec((tm, 2 * HEAD_DIM), rope_map),
            pl.BlockSpec((1, HEAD_DIM), lambda i, j: (0, 0)),
            pl.BlockSpec((1, HEAD_DIM), lambda i, j: (0, 0)),
        ],
        out_specs=pl.BlockSpec((tm, tn), lambda i, j: (i, j)),
        out_shape=jax.ShapeDtypeStruct((t, d_in), F32),
        scratch_shapes=[pltpu.VMEM((tm, d), BF16)],
        compiler_params=_cparams("parallel", "arbitrary"),
        name="in_projection",
    )(x, g.reshape(1, d), w, rope_tab, qg.reshape(1, HEAD_DIM), kg.reshape(1, HEAD_DIM))


LOG2E = 1.4426950408889634
SB_SCALE2 = HEAD_DIM ** -0.5 * LOG2E


def _sb_block(z2, valid, tri):
    soft = jnp.log2(1.0 + jnp.exp2(-jnp.abs(z2)))
    log_beta = jnp.minimum(z2, 0.0) - soft
    log_1m = -jnp.maximum(z2, 0.0) - soft
    if valid is not None:
        log_1m = jnp.where(valid, log_1m, 0.0)
    hi = log_1m.astype(BF16)
    lo = (log_1m - hi.astype(F32)).astype(BF16)
    logw = log_beta + (_dot(hi, tri) + _dot(lo, tri))
    if valid is not None:
        logw = jnp.where(valid, logw, NEG_INF)
    return logw, jnp.sum(log_1m, axis=-1, keepdims=True)


def _tri(n):
    idx = jnp.arange(n)
    return (idx[:, None] > idx[None, :]).astype(BF16)


def _sb_prompt_kernel(bias_ref, q_ref, k_ref, v_ref, tri_ref, o_ref, kb_ref, vb_ref, *, tq, hg):
    h0 = pl.program_id(1) * hg
    qi = pl.program_id(2)

    @pl.when(qi == 0)
    def _():
        kb_ref[...] = k_ref[...].astype(BF16)
        vb_ref[...] = v_ref[...].astype(BF16)

    lanes = [slice(j * HEAD_DIM, (j + 1) * HEAD_DIM) for j in range(hg)]
    q = [q_ref[:, sl].astype(BF16) for sl in lanes]
    bias2 = [bias_ref[h0 + j] * LOG2E for j in range(hg)]
    tri = tri_ref[...]
    row = lax.broadcasted_iota(jnp.int32, (tq, tq), 0)
    col = lax.broadcasted_iota(jnp.int32, (tq, tq), 1)

    def keys(ref, kb, j):
        return ref[pl.ds(pl.multiple_of(kb * tq, tq), tq), lanes[j]]

    def scores(kb, valid):
        return tuple(_sb_block(_dot_nt(q[j], keys(kb_ref, kb, j)) * SB_SCALE2 + bias2[j], valid, tri)
                     for j in range(hg))

    def attend(kb, blk, state):
        new = []
        for j in range(hg):
            (logw, rowsum), (carry, acc) = blk[j], state[j]
            a = jnp.exp2(logw + carry).astype(BF16)
            new.append((carry + rowsum, acc + _dot(a, keys(vb_ref, kb, j))))
        return tuple(new)

    def body(it, carried):
        blk, state = carried
        return scores(qi - 1 - it, None), attend(qi - it, blk, state)

    state = tuple((jnp.zeros((tq, 1), F32), jnp.zeros((tq, HEAD_DIM), F32)) for _ in range(hg))
    blk = scores(qi, col < row)
    blk, state = lax.fori_loop(0, qi, body, (blk, state))
    state = attend(0, blk, state)
    for j in range(hg):
        o_ref[:, lanes[j]] = state[j][1].astype(o_ref.dtype)


def _sb_prompt(p, bias, *, batch, seq, tq, hg):
    nq = seq // tq
    wide = hg * HEAD_DIM
    return pl.pallas_call(
        functools.partial(_sb_prompt_kernel, tq=tq, hg=hg),
        grid_spec=pltpu.PrefetchScalarGridSpec(
            num_scalar_prefetch=0,
            grid=(batch, N_HEADS // hg, nq),
            in_specs=[
                pl.BlockSpec(memory_space=pltpu.SMEM),
                pl.BlockSpec((tq, wide), lambda b, h, qi: (b * nq + qi, SEC_SQ // hg + h)),
                pl.BlockSpec((seq, wide), lambda b, h, qi: (b, SEC_SK // hg + h)),
                pl.BlockSpec((seq, wide), lambda b, h, qi: (b, SEC_SV // hg + h)),
                pl.BlockSpec((tq, tq), lambda b, h, qi: (0, 0)),
            ],
            out_specs=pl.BlockSpec((tq, wide), lambda b, h, qi: (b * nq + qi, h)),
            scratch_shapes=[pltpu.VMEM((seq, wide), BF16), pltpu.VMEM((seq, wide), BF16)],
        ),
        out_shape=jax.ShapeDtypeStruct((batch * seq, WIDTH), BF16),
        compiler_params=_cparams("parallel", "parallel", "arbitrary"),
        name="sb_prompt",
    )(bias, p, p, p, _tri(tq))


def _sb_decode_kernel(pt_ref, q_ref, kn_ref, vn_ref, *rest, n_group, dec_seq):
    k_refs = rest[:n_group]
    v_refs = rest[n_group:2 * n_group]
    bias_ref, tri_ref, o_ref, kpad_ref, vpad_ref, carry_ref, acc_ref = rest[2 * n_group:]
    g = pl.program_id(1)
    rows = N_HEADS * dec_seq
    tri = tri_ref[...]
    bias2 = bias_ref[...] * LOG2E
    qs = [q_ref[:, h * HEAD_DIM:(h + 1) * HEAD_DIM].astype(BF16) for h in range(N_HEADS)]

    def head_rows(ref, h):
        return ref[pl.ds(h, PAGE, stride=N_HEADS), :].astype(BF16)

    def process(pages, valid):
        n = len(pages)
        z = jnp.concatenate([_dot_nt(qs[h], head_rows(k_ref, h)) for k_ref, _ in pages for h in range(N_HEADS)],
                            axis=0)
        logw, rowsum = _sb_block(z * SB_SCALE2 + jnp.concatenate([bias2] * n, axis=0), valid, tri)
        carry = carry_ref[...]
        carries = []
        for r in range(n):
            carries.append(carry)
            carry = carry + rowsum[r * rows:(r + 1) * rows]
        carry_ref[...] = carry
        a = jnp.exp2(logw + jnp.concatenate(carries, axis=0))
        for h in range(N_HEADS):
            sl = slice(h * HEAD_DIM, (h + 1) * HEAD_DIM)
            acc = acc_ref[:, sl]
            for r, (_, v_ref) in enumerate(pages):
                ah = a[r * rows + h * dec_seq:r * rows + (h + 1) * dec_seq, :].astype(BF16)
                acc = acc + _dot(ah, head_rows(v_ref, h))
            acc_ref[:, sl] = acc

    @pl.when(g == 0)
    def _():
        carry_ref[...] = jnp.zeros_like(carry_ref)
        acc_ref[...] = jnp.zeros_like(acc_ref)
        kpad_ref[...] = jnp.zeros_like(kpad_ref)
        vpad_ref[...] = jnp.zeros_like(vpad_ref)
        for h in range(N_HEADS):
            sl = slice(h * HEAD_DIM, (h + 1) * HEAD_DIM)
            kpad_ref[pl.ds(h, dec_seq, stride=N_HEADS), :] = kn_ref[:, sl]
            vpad_ref[pl.ds(h, dec_seq, stride=N_HEADS), :] = vn_ref[:, sl]
        t_idx = lax.broadcasted_iota(jnp.int32, (rows, PAGE), 0) % dec_seq
        s_idx = lax.broadcasted_iota(jnp.int32, (rows, PAGE), 1)
        process([(kpad_ref, vpad_ref)], s_idx < t_idx)

    process([(k_refs[r], v_refs[r]) for r in range(n_group - 1, -1, -1)], None)

    @pl.when(g == pl.num_programs(1) - 1)
    def _():
        o_ref[...] = acc_ref[...]


def _sb_decode(p, cache_k, cache_v, page_table, bias, *, layer, n_prompt, dec_seq, n_group):
    dec_batch, n_pages = page_table.shape
    assert n_pages % n_group == 0 and dec_seq == 8
    n_steps = n_pages // n_group
    rows = N_HEADS * dec_seq
    q_row0 = n_prompt // dec_seq
    page_rows = PAGE * N_HEADS
    cache_k = cache_k.reshape(cache_k.shape[:2] + (page_rows, HEAD_DIM))
    cache_v = cache_v.reshape(cache_v.shape[:2] + (page_rows, HEAD_DIM))

    def page_map(r):
        def index(b, g, pt):
            return (layer, pt[b * n_pages + n_pages - (g + 1) * n_group + r], 0, 0)
        return index

    page_specs = [pl.BlockSpec((None, None, page_rows, HEAD_DIM), page_map(r)) for r in range(n_group)]
    sample_cols = lambda sec: pl.BlockSpec((dec_seq, WIDTH), lambda b, g, pt: (q_row0 + b, sec // N_HEADS))
    bias_rows = jnp.broadcast_to(jnp.repeat(bias.astype(F32), dec_seq)[:, None], (rows, PAGE))
    return pl.pallas_call(
        functools.partial(_sb_decode_kernel, n_group=n_group, dec_seq=dec_seq),
        grid_spec=pltpu.PrefetchScalarGridSpec(
            num_scalar_prefetch=1,
            grid=(dec_batch, n_steps),
            in_specs=[sample_cols(SEC_SQ), sample_cols(SEC_SK), sample_cols(SEC_SV)] + page_specs + page_specs + [
                pl.BlockSpec((rows, PAGE), lambda b, g, pt: (0, 0)),
                pl.BlockSpec((PAGE, PAGE), lambda b, g, pt: (0, 0)),
            ],
            out_specs=pl.BlockSpec((dec_seq, WIDTH), lambda b, g, pt: (b, 0)),
            scratch_shapes=[
                pltpu.VMEM((page_rows, HEAD_DIM), F32),
                pltpu.VMEM((page_rows, HEAD_DIM), F32),
                pltpu.VMEM((rows, 1), F32),
                pltpu.VMEM((dec_seq, WIDTH), F32),
            ],
        ),
        out_shape=jax.ShapeDtypeStruct((dec_batch * dec_seq, WIDTH), F32),
        compiler_params=_cparams("parallel", "arbitrary"),
        name="sb_decode",
    )(page_table.reshape(-1), p, p, p, *([cache_k] * n_group), *([cache_v] * n_group), bias_rows, _tri(PAGE))


def _ret_consts(log_gamma, chunk):
    idx = jnp.arange(chunk, dtype=F32)
    diff = idx[:, None] - idx[None, :]
    causal = diff >= 0
    decay = jnp.where(causal[None], jnp.exp(jnp.where(causal, diff, 0.0)[None] * log_gamma[:, None, None]), 0.0)
    q_decay = jnp.exp((idx + 1.0)[None, :] * log_gamma[:, None])
    k_decay = jnp.exp((chunk - 1.0 - idx)[None, :] * log_gamma[:, None])
    lanes = (N_HEADS, chunk, HEAD_DIM)
    return (decay, jnp.broadcast_to(q_decay[:, :, None], lanes), jnp.broadcast_to(k_decay[:, :, None], lanes),
            jnp.exp(chunk * log_gamma))


def _ret_chunk(q, k, v, rg, state, decay, qd, kd, state_decay):
    qb, kb, vb = q.astype(BF16), k.astype(BF16), v.astype(BF16)
    scores = _dot_nt(qb, kb) * decay
    o = _dot(scores.astype(BF16), vb) + _dot(qb, state.astype(BF16)) * qd
    new_state = state_decay * state + lax.dot_general((k * kd).astype(BF16), vb, (((0,), (0,)), ((), ())),
                                                      preferred_element_type=F32)
    out = _rms_rows(o) * (rg * _sigmoid(rg))
    return out, new_state


def _ret_prompt_kernel(sd_ref, q_ref, k_ref, v_ref, rg_ref, decay_ref, qd_ref, kd_ref, o_ref, s_ref, *,
                       chunk, n_chunks, hg):
    h0 = pl.program_id(1) * hg
    s_ref[...] = jnp.zeros_like(s_ref)

    def body(c, _):
        rows = pl.ds(pl.multiple_of(c * chunk, chunk), chunk)
        for j in range(hg):
            sl = slice(j * HEAD_DIM, (j + 1) * HEAD_DIM)
            out, new_state = _ret_chunk(q_ref[rows, sl], k_ref[rows, sl], v_ref[rows, sl], rg_ref[rows, sl],
                                        s_ref[j], decay_ref[j], qd_ref[j], kd_ref[j], sd_ref[h0 + j])
            s_ref[j] = new_state
            o_ref[rows, sl] = out.astype(o_ref.dtype)
        return 0

    lax.fori_loop(0, n_chunks, body, 0)


def _ret_prompt(p, log_gamma, *, batch, seq, chunk, hg):
    decay, qd, kd, sd = _ret_consts(log_gamma, chunk)
    wide = hg * HEAD_DIM
    col = lambda sec: pl.BlockSpec((seq, wide), lambda b, h: (b, sec // hg + h))
    per_head = lambda n: pl.BlockSpec((hg, chunk, n), lambda b, h: (h, 0, 0))
    return pl.pallas_call(
        functools.partial(_ret_prompt_kernel, chunk=chunk, n_chunks=seq // chunk, hg=hg),
        grid=(batch, N_HEADS // hg),
        in_specs=[pl.BlockSpec(memory_space=pltpu.SMEM), col(SEC_RQ), col(SEC_RK), col(SEC_RV), col(SEC_RG),
                  per_head(chunk), per_head(HEAD_DIM), per_head(HEAD_DIM)],
        out_specs=[pl.BlockSpec((seq, wide), lambda b, h: (b, h)),
                   pl.BlockSpec((None, hg, HEAD_DIM, HEAD_DIM), lambda b, h: (b, h, 0, 0))],
        out_shape=[jax.ShapeDtypeStruct((batch * seq, WIDTH), BF16),
                   jax.ShapeDtypeStruct((batch, N_HEADS, HEAD_DIM, HEAD_DIM), F32)],
        compiler_params=_cparams("parallel", "parallel"),
        name="retention_prompt",
    )(sd, p, p, p, p, decay, qd, kd)


def _ret_sample_kernel(sd_ref, q_ref, k_ref, v_ref, rg_ref, s_in_ref, decay_ref, qd_ref, kd_ref, o_ref, s_out_ref):
    for h in range(N_HEADS):
        sl = slice(h * HEAD_DIM, (h + 1) * HEAD_DIM)
        out, new_state = _ret_chunk(q_ref[:, sl], k_ref[:, sl], v_ref[:, sl], rg_ref[:, sl], s_in_ref[h],
                                    decay_ref[h], qd_ref[h], kd_ref[h], sd_ref[h])
        o_ref[:, sl] = out
        s_out_ref[h] = new_state


def _ret_sample(p, state, log_gamma, *, n_prompt, dec_seq):
    dec_batch = state.shape[0]
    decay, qd, kd, sd = _ret_consts(log_gamma, dec_seq)
    row0 = n_prompt // dec_seq
    col = lambda sec: pl.BlockSpec((dec_seq, WIDTH), lambda b: (row0 + b, sec // N_HEADS))
    full = lambda a: pl.BlockSpec(a.shape, lambda b: (0,) * a.ndim)
    state_spec = pl.BlockSpec((None, N_HEADS, HEAD_DIM, HEAD_DIM), lambda b: (b, 0, 0, 0))
    return pl.pallas_call(
        _ret_sample_kernel,
        grid=(dec_batch,),
        in_specs=[pl.BlockSpec(memory_space=pltpu.SMEM), col(SEC_RQ), col(SEC_RK), col(SEC_RV), col(SEC_RG),
                  state_spec, full(decay), full(qd), full(kd)],
        out_specs=[pl.BlockSpec((dec_seq, WIDTH), lambda b: (b, 0)), state_spec],
        out_shape=[jax.ShapeDtypeStruct((dec_batch * dec_seq, WIDTH), F32),
                   jax.ShapeDtypeStruct(state.shape, F32)],
        compiler_params=_cparams("parallel"),
        name="retention_sample",
    )(sd, p, p, p, p, state, decay, qd, kd)


def _merge_kernel(ap_ref, as_ref, bp_ref, bs_ref, wa_ref, wb_ref, ga_ref, gb_ref, o_ref, *, prompt_tiles):
    def emit(a_ref, b_ref):
        ya = _dot(a_ref[...], wa_ref[...])
        yb = _dot(b_ref[...], wb_ref[...])
        o_ref[...] = (_sigmoid(ga_ref[...]) * ya + _sigmoid(gb_ref[...]) * yb).astype(o_ref.dtype)

    @pl.when(pl.program_id(0) < prompt_tiles)
    def _():
        emit(ap_ref, bp_ref)

    @pl.when(pl.program_id(0) >= prompt_tiles)
    def _():
        emit(as_ref, bs_ref)


def _merge(sb_p, sb_s, ret_p, ret_s, w_pa, w_pb, p, *, tm, tn):
    t = p.shape[0]
    d = w_pa.shape[1]
    ga0 = COL_GA // tn
    gb0 = (COL_GA + d) // tn
    assert sb_p.shape[0] % tm == 0 and sb_s.shape[0] % tm == 0
    prompt_tiles = sb_p.shape[0] // tm
    prompt_rows = pl.BlockSpec((tm, WIDTH), lambda i, j: (jnp.minimum(i, prompt_tiles - 1), 0))
    sample_rows = pl.BlockSpec((tm, WIDTH), lambda i, j: (jnp.maximum(i - prompt_tiles, 0), 0))
    return pl.pallas_call(
        functools.partial(_merge_kernel, prompt_tiles=prompt_tiles),
        grid=(t // tm, d // tn),
        in_specs=[
            prompt_rows, sample_rows, prompt_rows, sample_rows,
            pl.BlockSpec((WIDTH, tn), lambda i, j: (0, j)),
            pl.BlockSpec((WIDTH, tn), lambda i, j: (0, j)),
            pl.BlockSpec((tm, tn), lambda i, j: (i, ga0 + j)),
            pl.BlockSpec((tm, tn), lambda i, j: (i, gb0 + j)),
        ],
        out_specs=pl.BlockSpec((tm, tn), lambda i, j: (i, j)),
        out_shape=jax.ShapeDtypeStruct((t, d), BF16),
        compiler_params=_cparams("parallel", "arbitrary"),
        name="branch_merge",
    )(sb_p, sb_s, ret_p, ret_s, w_pa, w_pb, p, p)


def _out_proj_kernel(m_ref, w_ref, x_ref, o_ref):
    o_ref[...] = x_ref[...] + _dot(m_ref[...], w_ref[...])


def _out_proj(m, w_o, x, *, tm, tn):
    t, d = x.shape
    return pl.pallas_call(
        _out_proj_kernel,
        grid=(t // tm, d // tn),
        in_specs=[
            pl.BlockSpec((tm, d), lambda i, j: (i, 0)),
            pl.BlockSpec((d, tn), lambda i, j: (0, j)),
            pl.BlockSpec((tm, tn), lambda i, j: (i, j)),
        ],
        out_specs=pl.BlockSpec((tm, tn), lambda i, j: (i, j)),
        out_shape=jax.ShapeDtypeStruct((t, d), F32),
        compiler_params=_cparams("parallel", "arbitrary"),
        name="out_projection",
    )(m, w_o, x)


def _swiglu(h, wg, wu, wd):
    gate = _dot(h, wg)
    up = _dot(h, wu)
    return _dot((gate * _sigmoid(gate) * up).astype(BF16), wd)


def _dense_ffn_kernel(x_ref, g_ref, wg_ref, wu_ref, wd_ref, o_ref, h_ref):
    @pl.when(pl.program_id(1) == 0)
    def _():
        x = x_ref[...]
        h_ref[...] = (_rms_rows(x) * g_ref[...]).astype(BF16)
        o_ref[...] = x

    o_ref[...] += _swiglu(h_ref[...], wg_ref[...], wu_ref[...], wd_ref[...])


def _dense_ffn(x, g, wg, wu, wd, *, tm, tf):
    t, d = x.shape
    d_ff = wg.shape[1]
    return pl.pallas_call(
        _dense_ffn_kernel,
        grid=(t // tm, d_ff // tf),
        in_specs=[
            pl.BlockSpec((tm, d), lambda i, f: (i, 0), pipeline_mode=pl.Buffered(1)),
            pl.BlockSpec((1, d), lambda i, f: (0, 0)),
            pl.BlockSpec((d, tf), lambda i, f: (0, f)),
            pl.BlockSpec((d, tf), lambda i, f: (0, f)),
            pl.BlockSpec((tf, d), lambda i, f: (f, 0)),
        ],
        out_specs=pl.BlockSpec((tm, d), lambda i, f: (i, 0)),
        out_shape=jax.ShapeDtypeStruct((t, d), F32),
        scratch_shapes=[pltpu.VMEM((tm, d), BF16)],
        compiler_params=_cparams("parallel", "arbitrary"),
        name="dense_ffn",
    )(x, g.reshape(1, d), wg, wu, wd)


def _expert_ffn_kernel(te_ref, tr_ref, h_ref, wg_ref, wu_ref, wd_ref, o_ref, wgb_ref, wub_ref, wdb_ref, *, sub):
    i = pl.program_id(0)
    n_sub = (tr_ref[i] + (sub - 1)) // sub

    @pl.when(pl.program_id(1) == 0)
    def _():
        o_ref[...] = jnp.zeros_like(o_ref)

    @pl.when(n_sub > 0)
    def _():
        wgb_ref[...] = wg_ref[...].astype(BF16)
        wub_ref[...] = wu_ref[...].astype(BF16)
        wdb_ref[...] = wd_ref[...].astype(BF16)

        def run(first, count):
            rs = pl.ds(pl.multiple_of(first * sub, sub), count * sub)
            o_ref[rs, :] += _swiglu(h_ref[rs, :], wgb_ref[...], wub_ref[...], wdb_ref[...])

        def pair(s, _):
            run(2 * s, 2)
            return 0

        lax.fori_loop(0, n_sub // 2, pair, 0)

        @pl.when(n_sub % 2 == 1)
        def _():
            run(n_sub - 1, 1)


def _expert_ffn(hs, wg, wu, wd, tile_expert, tile_rows, *, tm, tf, sub):
    r, d = hs.shape
    d_ff = wg.shape[2]
    nf = d_ff // tf

    def f_idx(i, f, tr):
        return jnp.where(tr[i] > 0, f, nf - 1)

    return pl.pallas_call(
        functools.partial(_expert_ffn_kernel, sub=sub),
        grid_spec=pltpu.PrefetchScalarGridSpec(
            num_scalar_prefetch=2,
            grid=(r // tm, nf),
            in_specs=[
                pl.BlockSpec((tm, d), lambda i, f, te, tr: (i, 0), pipeline_mode=pl.Buffered(1)),
                pl.BlockSpec((None, d, tf), lambda i, f, te, tr: (te[i], 0, f_idx(i, f, tr))),
                pl.BlockSpec((None, d, tf), lambda i, f, te, tr: (te[i], 0, f_idx(i, f, tr))),
                pl.BlockSpec((None, tf, d), lambda i, f, te, tr: (te[i], f_idx(i, f, tr), 0)),
            ],
            out_specs=pl.BlockSpec((tm, d), lambda i, f, te, tr: (i, 0)),
            scratch_shapes=[pltpu.VMEM((d, tf), BF16), pltpu.VMEM((d, tf), BF16), pltpu.VMEM((tf, d), BF16)],
        ),
        out_shape=jax.ShapeDtypeStruct((r, d), F32),
        compiler_params=_cparams("parallel", "arbitrary"),
        name="expert_ffn",
    )(tile_expert, tile_rows, hs, wg, wu, wd)


def _router_kernel(x_ref, g_ref, w_ref, o_ref, hn_ref, *, n_experts):
    h = _rms_rows(x_ref[...]) * g_ref[...]
    hn_ref[...] = h
    w = w_ref[...]
    h_hi = h.astype(BF16)
    h_lo = (h - h_hi.astype(F32)).astype(BF16)
    w_hi = w.astype(BF16)
    w_lo = (w - w_hi.astype(F32)).astype(BF16)
    logits = _dot(h_hi, w_hi) + (_dot(h_hi, w_lo) + _dot(h_lo, w_hi))
    lane = lax.broadcasted_iota(jnp.int32, logits.shape, 1).astype(F32)
    lg = jnp.where(lane < n_experts, logits, NEG_INF)
    m1 = jnp.max(lg, axis=-1, keepdims=True)
    i1 = jnp.min(jnp.where(lg == m1, lane, float(HEAD_DIM)), axis=-1, keepdims=True)
    lg2 = jnp.where(lane == i1, NEG_INF, lg)
    m2 = jnp.max(lg2, axis=-1, keepdims=True)
    i2 = jnp.min(jnp.where(lg2 == m2, lane, float(HEAD_DIM)), axis=-1, keepdims=True)
    e = jnp.exp(m2 - m1)
    g1 = 1.0 / (1.0 + e)
    g2 = e / (1.0 + e)
    o_ref[...] = jnp.where(lane == 0, i1, jnp.where(lane == 1, i2, jnp.where(lane == 2, g1, jnp.where(lane == 3, g2, 0.0))))


def _router(x, g, w_router, *, tm):
    t, d = x.shape
    n_experts = w_router.shape[1]
    w_pad = jnp.pad(w_router, ((0, 0), (0, HEAD_DIM - n_experts)))
    return pl.pallas_call(
        functools.partial(_router_kernel, n_experts=n_experts),
        grid=(t // tm,),
        in_specs=[
            pl.BlockSpec((tm, d), lambda i: (i, 0)),
            pl.BlockSpec((1, d), lambda i: (0, 0)),
            pl.BlockSpec((d, HEAD_DIM), lambda i: (0, 0)),
        ],
        out_specs=[pl.BlockSpec((tm, HEAD_DIM), lambda i: (i, 0)), pl.BlockSpec((tm, d), lambda i: (i, 0))],
        out_shape=[jax.ShapeDtypeStruct((t, HEAD_DIM), F32), jax.ShapeDtypeStruct((t, d), F32)],
        compiler_params=_cparams("parallel"),
        name="router",
    )(x, g.reshape(1, d), w_pad)


def _row_copy(src_hbm, dst_ref, sem, src_row, dst_row):
    return pltpu.make_async_copy(src_hbm.at[pl.ds(src_row, 1)], dst_ref.at[pl.ds(dst_row, 1)], sem)


def _gather_rows_kernel(idx_ref, idx_next_ref, hn_hbm, o_ref, buf_ref, sem, *, rows):
    i = pl.program_id(0)
    slot = i % 2

    def start_all(ids_ref, s):
        def start(r, _):
            _row_copy(hn_hbm, buf_ref.at[s], sem.at[s], ids_ref[0, r], r).start()
            return 0
        lax.fori_loop(0, rows, start, 0, unroll=8)

    @pl.when(i == 0)
    def _():
        start_all(idx_ref, slot)

    @pl.when(i + 1 < pl.num_programs(0))
    def _():
        start_all(idx_next_ref, 1 - slot)

    def wait(r, _):
        _row_copy(hn_hbm, buf_ref.at[slot], sem.at[slot], 0, r).wait()
        return 0

    lax.fori_loop(0, rows, wait, 0, unroll=8)
    o_ref[...] = buf_ref[slot].astype(BF16)


def _gather_rows(hn, row_token, *, rows):
    r = row_token.shape[0]
    d = hn.shape[1]
    steps = r // rows
    ids = row_token.reshape(steps, 1, rows)
    return pl.pallas_call(
        functools.partial(_gather_rows_kernel, rows=rows),
        grid=(steps,),
        in_specs=[
            pl.BlockSpec((None, 1, rows), lambda i: (i, 0, 0), memory_space=pltpu.SMEM),
            pl.BlockSpec((None, 1, rows), lambda i: (jnp.minimum(i + 1, steps - 1), 0, 0), memory_space=pltpu.SMEM),
            pl.BlockSpec(memory_space=pl.ANY),
        ],
        out_specs=pl.BlockSpec((rows, d), lambda i: (i, 0)),
        out_shape=jax.ShapeDtypeStruct((r, d), BF16),
        scratch_shapes=[pltpu.VMEM((2, rows, d), F32), pltpu.SemaphoreType.DMA((2,))],
        compiler_params=_cparams("arbitrary"),
        name="gather_expert_rows",
    )(ids, ids, hn)


def _combine_kernel(pos_ref, pos_next_ref, x_ref, route_ref, y_hbm, op_ref, os_ref, ya_ref, yb_ref, sem, *,
                    rows, prompt_steps):
    i = pl.program_id(0)
    slot = i % 2

    def start_all(p_ref, s):
        def start(r, _):
            _row_copy(y_hbm, ya_ref.at[s], sem.at[s], p_ref[0, r], r).start()
            _row_copy(y_hbm, yb_ref.at[s], sem.at[s], p_ref[0, rows + r], r).start()
            return 0
        lax.fori_loop(0, rows, start, 0, unroll=8)

    @pl.when(i == 0)
    def _():
        start_all(pos_ref, slot)

    @pl.when(i + 1 < pl.num_programs(0))
    def _():
        start_all(pos_next_ref, 1 - slot)

    def wait(r, _):
        _row_copy(y_hbm, ya_ref.at[slot], sem.at[slot], 0, r).wait()
        _row_copy(y_hbm, yb_ref.at[slot], sem.at[slot], 0, r).wait()
        return 0

    lax.fori_loop(0, rows, wait, 0, unroll=8)
    route = route_ref[...]
    out = x_ref[...] + route[:, 2:3] * ya_ref[slot] + route[:, 3:4] * yb_ref[slot]

    @pl.when(i < prompt_steps)
    def _():
        op_ref[...] = out

    @pl.when(i >= prompt_steps)
    def _():
        os_ref[...] = out


def _combine(x, route, y_sorted, pos, *, rows, n_prompt):
    t, d = x.shape
    assert n_prompt % rows == 0 and 0 < n_prompt < t
    prompt_steps = n_prompt // rows
    steps = t // rows
    return pl.pallas_call(
        functools.partial(_combine_kernel, rows=rows, prompt_steps=prompt_steps),
        grid=(steps,),
        in_specs=[
            pl.BlockSpec((None, 1, TOP_K * rows), lambda i: (i, 0, 0), memory_space=pltpu.SMEM),
            pl.BlockSpec((None, 1, TOP_K * rows), lambda i: (jnp.minimum(i + 1, steps - 1), 0, 0),
                         memory_space=pltpu.SMEM),
            pl.BlockSpec((rows, d), lambda i: (i, 0)),
            pl.BlockSpec((rows, HEAD_DIM), lambda i: (i, 0)),
            pl.BlockSpec(memory_space=pl.ANY),
        ],
        out_specs=[pl.BlockSpec((rows, d), lambda i: (jnp.minimum(i, prompt_steps - 1), 0)),
                   pl.BlockSpec((rows, d), lambda i: (jnp.maximum(i - prompt_steps, 0), 0))],
        out_shape=[jax.ShapeDtypeStruct((n_prompt, d), F32), jax.ShapeDtypeStruct((t - n_prompt, d), F32)],
        scratch_shapes=[pltpu.VMEM((2, rows, d), F32), pltpu.VMEM((2, rows, d), F32), pltpu.SemaphoreType.DMA((2,))],
        compiler_params=_cparams("arbitrary"),
        name="combine_expert_rows",
    )(pos, pos, x, route, y_sorted)


def _routing_tables(route, n_experts, *, tm, rows):
    t = route.shape[0]
    expert = route[:, :TOP_K].astype(jnp.int32).reshape(-1)
    onehot = (expert[:, None] == jnp.arange(n_experts)[None, :]).astype(jnp.int32)
    rank = jnp.sum((jnp.cumsum(onehot, axis=0) - onehot) * onehot, axis=1)
    count = jnp.sum(onehot, axis=0)
    tiles = (count + tm - 1) // tm
    tile_end = jnp.cumsum(tiles)
    start = (tile_end - tiles) * tm
    dest = start[expert] + rank
    n_rows = (t * TOP_K // tm + n_experts) * tm
    n_tiles = n_rows // tm
    row_token = jnp.zeros((n_rows,), jnp.int32).at[dest].set(jnp.arange(t * TOP_K, dtype=jnp.int32) // TOP_K)
    tile_id = jnp.arange(n_tiles)
    tile_valid = tile_id < tile_end[-1]
    tile_expert = jnp.minimum(jnp.sum(tile_id[:, None] >= tile_end[None, :], axis=1), n_experts - 1)
    last_expert = tile_expert[jnp.maximum(tile_end[-1] - 1, 0)]
    first_tile = (tile_end - tiles)[tile_expert]
    tile_rows = jnp.clip(count[tile_expert] - (tile_id - first_tile) * tm, 0, tm)
    tile_rows = jnp.where(tile_valid, tile_rows, 0).astype(jnp.int32)
    tile_expert = jnp.where(tile_valid, tile_expert, last_expert).astype(jnp.int32)
    dest = dest.reshape(t // rows, rows, TOP_K)
    pos = jnp.concatenate([dest[:, :, 0], dest[:, :, 1]], axis=1).reshape(t // rows, 1, TOP_K * rows)
    return row_token, tile_expert, tile_rows, pos


def _moe(x, g, w_router, wg, wu, wd, *, n_prompt, tm_route, tm, tf, sub, rows):
    n_experts = w_router.shape[1]
    route, hn = _router(x, g, w_router, tm=tm_route)
    row_token, tile_expert, tile_rows, pos = _routing_tables(route, n_experts, tm=tm, rows=rows)
    hs = _gather_rows(hn, row_token, rows=rows)
    ys = _expert_ffn(hs, wg, wu, wd, tile_expert, tile_rows, tm=tm, tf=tf, sub=sub)
    return _combine(x, route, ys, pos, rows=rows, n_prompt=n_prompt)


def _kv_rows_kernel(*refs, depth, tr, prompt_steps):
    ins, (kp_ref, vp_ref, ks_ref, vs_ref) = refs[:2 * depth], refs[2 * depth:]
    l = pl.program_id(0)
    i = pl.program_id(1)

    def emit(src_ref, dst_ref):
        for h in range(N_HEADS):
            dst_ref[pl.ds(h, tr, stride=N_HEADS), :] = src_ref[:, h * HEAD_DIM:(h + 1) * HEAD_DIM]

    for layer in range(depth):
        @pl.when(jnp.logical_and(l == layer, i < prompt_steps))
        def _():
            emit(ins[2 * layer], kp_ref)
            emit(ins[2 * layer + 1], vp_ref)

        @pl.when(jnp.logical_and(l == layer, i >= prompt_steps))
        def _():
            emit(ins[2 * layer], ks_ref)
            emit(ins[2 * layer + 1], vs_ref)


def _kv_rows(projections, *, n_prompt, tr):
    depth = len(projections)
    t = projections[0].shape[0]
    assert n_prompt % tr == 0 and (t - n_prompt) % tr == 0
    steps = t // tr
    prompt_steps = n_prompt // tr

    def src(layer, sec):
        def index(l, i):
            return (jnp.where(l == layer, i, jnp.where(l < layer, 0, steps - 1)), sec // N_HEADS)
        return pl.BlockSpec((tr, WIDTH), index)

    prompt_spec = pl.BlockSpec((None, tr * N_HEADS, HEAD_DIM), lambda l, i: (l, jnp.minimum(i, prompt_steps - 1), 0))
    sample_spec = pl.BlockSpec((None, tr * N_HEADS, HEAD_DIM), lambda l, i: (l, jnp.maximum(i - prompt_steps, 0), 0))
    prompt_shape = jax.ShapeDtypeStruct((depth, n_prompt * N_HEADS, HEAD_DIM), F32)
    sample_shape = jax.ShapeDtypeStruct((depth, (t - n_prompt) * N_HEADS, HEAD_DIM), F32)
    return pl.pallas_call(
        functools.partial(_kv_rows_kernel, depth=depth, tr=tr, prompt_steps=prompt_steps),
        grid=(depth, steps),
        in_specs=[src(layer, sec) for layer in range(depth) for sec in (SEC_SK, SEC_SV)],
        out_specs=[prompt_spec, prompt_spec, sample_spec, sample_spec],
        out_shape=[prompt_shape, prompt_shape, sample_shape, sample_shape],
        compiler_params=_cparams("arbitrary", "arbitrary"),
        name="kv_rows",
    )(*[p for p in projections for _ in range(2)])


def _pick(n, want):
    t = min(n, want)
    while n % t:
        t -= 8
    return t


def kernel(x_prompt, x_sample, cache_sb_k, cache_sb_v, state_ret, page_table, norm_attn, w_in, qnorm_g, knorm_g,
           sb_bias, w_pa, w_pb, w_o, norm_ffn, w_ff_gate, w_ff_up, w_ff_down, w_router, w_exp_gate, w_exp_up,
           w_exp_down):
    batch, seq, d = x_prompt.shape
    dec_batch, dec_seq, _ = x_sample.shape
    depth = w_in.shape[0]
    n_pages = page_table.shape[1]
    past_len = n_pages * cache_sb_k.shape[2]
    n_prompt = batch * seq
    n_sample = dec_batch * dec_seq
    t = n_prompt + n_sample
    log_gamma = jnp.log1p(-jnp.exp2(-5.0 - jnp.arange(N_HEADS, dtype=F32)))

    tm = _pick(int(np.gcd(seq, n_sample)), 1024)
    tq = _pick(seq, 256)
    chunk = _pick(seq, 128)
    tm_moe = _pick(TOP_K * t, 1024)
    rope_tab = _rope_table(seq, dec_seq, past_len, tm)
    w_in, w_pa, w_pb, w_o, w_ff_gate, w_ff_up, w_ff_down = (
        w.astype(BF16) for w in (w_in, w_pa, w_pb, w_o, w_ff_gate, w_ff_up, w_ff_down))

    x = jnp.concatenate([x_prompt.reshape(n_prompt, d), x_sample.reshape(n_sample, d)], axis=0)
    projections, sp, ss = [], [], []
    y_prompt = y_sample = None
    for l in range(depth):
        p = _in_projection(x, norm_attn[l], w_in[l], rope_tab, qnorm_g[l], knorm_g[l],
                           n_prompt=n_prompt, seq=seq, tm=tm, tn=1024)
        projections.append(p)

        o_sb_p = _sb_prompt(p, sb_bias[l], batch=batch, seq=seq, tq=tq, hg=4)
        o_sb_s = _sb_decode(p, cache_sb_k, cache_sb_v, page_table, sb_bias[l], layer=l, n_prompt=n_prompt,
                            dec_seq=dec_seq, n_group=_pick(n_pages * 8, 64) // 8)
        ret_p, state_p = _ret_prompt(p, log_gamma, batch=batch, seq=seq, chunk=chunk, hg=4)
        ret_s, state_s = _ret_sample(p, state_ret[l], log_gamma, n_prompt=n_prompt, dec_seq=dec_seq)
        sp.append(state_p)
        ss.append(state_s)

        m = _merge(o_sb_p, o_sb_s.astype(BF16), ret_p, ret_s.astype(BF16), w_pa[l], w_pb[l], p, tm=tm, tn=1024)
        x = _out_proj(m, w_o[l], x, tm=tm, tn=1024)

        i = l // 2
        if l % 2 == 0:
            x = _dense_ffn(x, norm_ffn[l], w_ff_gate[i], w_ff_up[i], w_ff_down[i], tm=tm, tf=512)
        else:
            y_prompt, y_sample = _moe(x, norm_ffn[l], w_router[i], w_exp_gate[i], w_exp_up[i], w_exp_down[i],
                                      n_prompt=n_prompt, tm_route=_pick(t, 512), tm=tm_moe, tf=512,
                                      sub=_pick(tm_moe, 256), rows=_pick(int(np.gcd(tm_moe, n_sample)), 128))
            if l + 1 < depth:
                x = jnp.concatenate([y_prompt, y_sample], axis=0)
    if depth % 2:
        y_prompt, y_sample = x[:n_prompt], x[n_prompt:]

    kp, vp, ks, vs = _kv_rows(projections, n_prompt=n_prompt, tr=_pick(int(np.gcd(n_prompt, n_sample)), 256))
    rows_p = (depth, batch, seq, N_HEADS, HEAD_DIM)
    rows_s = (depth, dec_batch, dec_seq, N_HEADS, HEAD_DIM)
    return (y_prompt.reshape(batch, seq, d), y_sample.reshape(dec_batch, dec_seq, d),
            kp.reshape(rows_p), vp.reshape(rows_p), jnp.stack(sp), ks.reshape(rows_s), vs.reshape(rows_s), jnp.stack(ss))
```

```python
import functools

import jax
import jax.numpy as jnp
import numpy as np
from jax import lax
from jax.experimental import pallas as pl
from jax.experimental.pallas import tpu as pltpu

F32 = jnp.float32
BF16 = jnp.bfloat16

HEAD_DIM = 128
N_HEADS = 8
WIDTH = N_HEADS * HEAD_DIM
PAGE = 128
TOP_K = 2
EPS = 1e-6
ROPE_BASE = 10000.0
NEG_INF = float("-inf")

SEC_SQ, SEC_SK, SEC_SV, SEC_RQ, SEC_RK, SEC_RV, SEC_RG = (s * N_HEADS for s in range(7))
COL_GA = 7 * WIDTH
D_IN_SECTIONS = 7

VMEM_LIMIT = 56 * 1024 * 1024


def _cparams(*semantics):
    return pltpu.CompilerParams(dimension_semantics=semantics, vmem_limit_bytes=VMEM_LIMIT)


def _dot(a, b):
    return jnp.dot(a, b, preferred_element_type=F32)


def _dot_nt(a, b):
    return lax.dot_general(a, b, (((1,), (1,)), ((), ())), preferred_element_type=F32)


def _sigmoid(x):
    return 1.0 / (1.0 + jnp.exp(-x))


def _rms_rows(x):
    return x * lax.rsqrt(jnp.mean(x * x, axis=-1, keepdims=True) + EPS)


def _inproj_kernel(x_ref, g_ref, w_ref, rope_ref, qg_ref, kg_ref, o_ref, h_ref, *, tn):
    j = pl.program_id(1)

    @pl.when(j == 0)
    def _():
        h_ref[...] = (_rms_rows(x_ref[...]) * g_ref[...]).astype(BF16)

    acc = _dot(h_ref[...], w_ref[...])
    sec = j // (WIDTH // tn)
    heads = tn // HEAD_DIM

    @pl.when(sec <= 1)
    def _():
        gain = jnp.where(sec == 0, qg_ref[...], kg_ref[...])
        for hh in range(heads):
            sl = slice(hh * HEAD_DIM, (hh + 1) * HEAD_DIM)
            o_ref[:, sl] = _rms_rows(acc[:, sl]) * gain

    @pl.when(jnp.logical_or(sec == 3, sec == 4))
    def _():
        cos = rope_ref[:, :HEAD_DIM]
        sin = rope_ref[:, HEAD_DIM:]
        scale = jnp.where(sec == 4, HEAD_DIM ** -0.5, 1.0).astype(F32)
        for hh in range(heads):
            sl = slice(hh * HEAD_DIM, (hh + 1) * HEAD_DIM)
            blk = acc[:, sl]
            o_ref[:, sl] = (blk * cos + pltpu.roll(blk, HEAD_DIM // 2, 1) * sin) * scale

    @pl.when(jnp.logical_and(sec != 0, jnp.logical_and(sec != 1, jnp.logical_and(sec != 3, sec != 4))))
    def _():
        o_ref[...] = acc


def _rope_table(seq, dec_seq, past_len, tm):
    half = HEAD_DIM // 2
    inv_freq = ROPE_BASE ** (-jnp.arange(half, dtype=F32) / half)
    pos = jnp.concatenate([jnp.arange(seq), past_len + (jnp.arange(tm) % dec_seq)]).astype(F32)
    ang = pos[:, None] * inv_freq[None, :]
    cos, sin = jnp.cos(ang), jnp.sin(ang)
    return jnp.concatenate([cos, cos, -sin, sin], axis=-1)


def _in_projection(x, g, w, rope_tab, qg, kg, *, n_prompt, seq, tm, tn):
    t, d = x.shape
    d_in = w.shape[1]
    assert t % tm == 0 and d_in % tn == 0 and WIDTH % tn == 0 and seq % tm == 0 and n_prompt % tm == 0
    n_prompt_tiles = n_prompt // tm
    per_seq = seq // tm

    def rope_map(i, j):
        return (jnp.where(i < n_prompt_tiles, i % per_seq, per_seq), 0)

    return pl.pallas_call(
        functools.partial(_inproj_kernel, tn=tn),
        grid=(t // tm, d_in // tn),
        in_specs=[
            pl.BlockSpec((tm, d), lambda i, j: (i, 0)),
            pl.BlockSpec((1, d), lambda i, j: (0, 0)),
            pl.BlockSpec((d, tn), lambda i, j: (0, j)),
            pl.BlockSpec((tm, 2 * HEAD_DIM), rope_map),
            pl.BlockSpec((1, HEAD_DIM), lambda i, j: (0, 0)),
            pl.BlockSpec((1, HEAD_DIM), lambda i, j: (0, 0)),
        ],
        out_specs=pl.BlockSpec((tm, tn), lambda i, j: (i, j)),
        out_shape=jax.ShapeDtypeStruct((t, d_in), F32),
        scratch_shapes=[pltpu.VMEM((tm, d), BF16)],
        compiler_params=_cparams("parallel", "arbitrary"),
        name="in_projection",
    )(x, g.reshape(1, d), w, rope_tab, qg.reshape(1, HEAD_DIM), kg.reshape(1, HEAD_DIM))


LOG2E = 1.4426950408889634
SB_SCALE2 = HEAD_DIM ** -0.5 * LOG2E


def _sb_block(z2, valid, tri):
    soft = jnp.log2(1.0 + jnp.exp2(-jnp.abs(z2)))
    log_beta = jnp.minimum(z2, 0.0) - soft
    log_1m = -jnp.maximum(z2, 0.0) - soft
    if valid is not None:
        log_1m = jnp.where(valid, log_1m, 0.0)
    hi = log_1m.astype(BF16)
    lo = (log_1m - hi.astype(F32)).astype(BF16)
    logw = log_beta + (_dot(hi, tri) + _dot(lo, tri))
    if valid is not None:
        logw = jnp.where(valid, logw, NEG_INF)
    return logw, jnp.sum(log_1m, axis=-1, keepdims=True)


def _tri(n):
    idx = jnp.arange(n)
    return (idx[:, None] > idx[None, :]).astype(BF16)


def _sb_prompt_kernel(bias_ref, q_ref, k_ref, v_ref, tri_ref, o_ref, kb_ref, vb_ref, *, tq, hg):
    h0 = pl.program_id(1) * hg
    qi = pl.program_id(2)

    @pl.when(qi == 0)
    def _():
        kb_ref[...] = k_ref[...].astype(BF16)
        vb_ref[...] = v_ref[...].astype(BF16)

    lanes = [slice(j * HEAD_DIM, (j + 1) * HEAD_DIM) for j in range(hg)]
    q = [q_ref[:, sl].astype(BF16) for sl in lanes]
    bias2 = [bias_ref[h0 + j] * LOG2E for j in range(hg)]
    tri = tri_ref[...]
    row = lax.broadcasted_iota(jnp.int32, (tq, tq), 0)
    col = lax.broadcasted_iota(jnp.int32, (tq, tq), 1)

    def keys(ref, kb, j):
        return ref[pl.ds(pl.multiple_of(kb * tq, tq), tq), lanes[j]]

    def scores(kb, valid):
        return tuple(_sb_block(_dot_nt(q[j], keys(kb_ref, kb, j)) * SB_SCALE2 + bias2[j], valid, tri)
                     for j in range(hg))

    def attend(kb, blk, state):
        new = []
        for j in range(hg):
            (logw, rowsum), (carry, acc) = blk[j], state[j]
            a = jnp.exp2(logw + carry).astype(BF16)
            new.append((carry + rowsum, acc + _dot(a, keys(vb_ref, kb, j))))
        return tuple(new)

    def body(it, carried):
        blk, state = carried
        return scores(qi - 1 - it, None), attend(qi - it, blk, state)

    state = tuple((jnp.zeros((tq, 1), F32), jnp.zeros((tq, HEAD_DIM), F32)) for _ in range(hg))
    blk = scores(qi, col < row)
    blk, state = lax.fori_loop(0, qi, body, (blk, state))
    state = attend(0, blk, state)
    for j in range(hg):
        o_ref[:, lanes[j]] = state[j][1].astype(o_ref.dtype)


def _sb_prompt(p, bias, *, batch, seq, tq, hg):
    nq = seq // tq
    wide = hg * HEAD_DIM
    return pl.pallas_call(
        functools.partial(_sb_prompt_kernel, tq=tq, hg=hg),
        grid_spec=pltpu.PrefetchScalarGridSpec(
            num_scalar_prefetch=0,
            grid=(batch, N_HEADS // hg, nq),
            in_specs=[
                pl.BlockSpec(memory_space=pltpu.SMEM),
                pl.BlockSpec((tq, wide), lambda b, h, qi: (b * nq + qi, SEC_SQ // hg + h)),
                pl.BlockSpec((seq, wide), lambda b, h, qi: (b, SEC_SK // hg + h)),
                pl.BlockSpec((seq, wide), lambda b, h, qi: (b, SEC_SV // hg + h)),
                pl.BlockSpec((tq, tq), lambda b, h, qi: (0, 0)),
            ],
            out_specs=pl.BlockSpec((tq, wide), lambda b, h, qi: (b * nq + qi, h)),
            scratch_shapes=[pltpu.VMEM((seq, wide), BF16), pltpu.VMEM((seq, wide), BF16)],
        ),
        out_shape=jax.ShapeDtypeStruct((batch * seq, WIDTH), BF16),
        compiler_params=_cparams("parallel", "parallel", "arbitrary"),
        name="sb_prompt",
    )(bias, p, p, p, _tri(tq))


def _sb_decode_kernel(pt_ref, q_ref, kn_ref, vn_ref, *rest, n_group, dec_seq):
    k_refs = rest[:n_group]
    v_refs = rest[n_group:2 * n_group]
    bias_ref, tri_ref, o_ref, kpad_ref, vpad_ref, carry_ref, acc_ref = rest[2 * n_group:]
    g = pl.program_id(1)
    rows = N_HEADS * dec_seq
    tri = tri_ref[...]
    bias2 = bias_ref[...] * LOG2E
    qs = [q_ref[:, h * HEAD_DIM:(h + 1) * HEAD_DIM].astype(BF16) for h in range(N_HEADS)]

    def head_rows(ref, h):
        return ref[pl.ds(h, PAGE, stride=N_HEADS), :].astype(BF16)

    def process(pages, valid):
        n = len(pages)
        z = jnp.concatenate([_dot_nt(qs[h], head_rows(k_ref, h)) for k_ref, _ in pages for h in range(N_HEADS)],
                            axis=0)
        logw, rowsum = _sb_block(z * SB_SCALE2 + jnp.concatenate([bias2] * n, axis=0), valid, tri)
        carry = carry_ref[...]
        carries = []
        for r in range(n):
            carries.append(carry)
            carry = carry + rowsum[r * rows:(r + 1) * rows]
        carry_ref[...] = carry
        a = jnp.exp2(logw + jnp.concatenate(carries, axis=0))
        for h in range(N_HEADS):
            sl = slice(h * HEAD_DIM, (h + 1) * HEAD_DIM)
            acc = acc_ref[:, sl]
            for r, (_, v_ref) in enumerate(pages):
                ah = a[r * rows + h * dec_seq:r * rows + (h + 1) * dec_seq, :].astype(BF16)
                acc = acc + _dot(ah, head_rows(v_ref, h))
            acc_ref[:, sl] = acc

    @pl.when(g == 0)
    def _():
        carry_ref[...] = jnp.zeros_like(carry_ref)
        acc_ref[...] = jnp.zeros_like(acc_ref)
        kpad_ref[...] = jnp.zeros_like(kpad_ref)
        vpad_ref[...] = jnp.zeros_like(vpad_ref)
        for h in range(N_HEADS):
            sl = slice(h * HEAD_DIM, (h + 1) * HEAD_DIM)
            kpad_ref[pl.ds(h, dec_seq, stride=N_HEADS), :] = kn_ref[:, sl]
            vpad_ref[pl.ds(h, dec_seq, stride=N_HEADS), :] = vn_ref[:, sl]
        t_idx = lax.broadcasted_iota(jnp.int32, (rows, PAGE), 0) % dec_seq
        s_idx = lax.broadcasted_iota(jnp.int32, (rows, PAGE), 1)
        process([(kpad_ref, vpad_ref)], s_idx < t_idx)

    process([(k_refs[r], v_refs[r]) for r in range(n_group - 1, -1, -1)], None)

    @pl.when(g == pl.num_programs(1) - 1)
    def _():
        o_ref[...] = acc_ref[...]


def _sb_decode(p, cache_k, cache_v, page_table, bias, *, layer, n_prompt, dec_seq, n_group):
    dec_batch, n_pages = page_table.shape
    assert n_pages % n_group == 0 and dec_seq == 8
    n_steps = n_pages // n_group
    rows = N_HEADS * dec_seq
    q_row0 = n_prompt // dec_seq
    page_rows = PAGE * N_HEADS
    cache_k = cache_k.reshape(cache_k.shape[:2] + (page_rows, HEAD_DIM))
    cache_v = cache_v.reshape(cache_v.shape[:2] + (page_rows, HEAD_DIM))

    def page_map(r):
        def index(b, g, pt):
            return (layer, pt[b * n_pages + n_pages - (g + 1) * n_group + r], 0, 0)
        return index

    page_specs = [pl.BlockSpec((None, None, page_rows, HEAD_DIM), page_map(r)) for r in range(n_group)]
    sample_cols = lambda sec: pl.BlockSpec((dec_seq, WIDTH), lambda b, g, pt: (q_row0 + b, sec // N_HEADS))
    bias_rows = jnp.broadcast_to(jnp.repeat(bias.astype(F32), dec_seq)[:, None], (rows, PAGE))
    return pl.pallas_call(
        functools.partial(_sb_decode_kernel, n_group=n_group, dec_seq=dec_seq),
        grid_spec=pltpu.PrefetchScalarGridSpec(
            num_scalar_prefetch=1,
            grid=(dec_batch, n_steps),
            in_specs=[sample_cols(SEC_SQ), sample_cols(SEC_SK), sample_cols(SEC_SV)] + page_specs + page_specs + [
                pl.BlockSpec((rows, PAGE), lambda b, g, pt: (0, 0)),
                pl.BlockSpec((PAGE, PAGE), lambda b, g, pt: (0, 0)),
            ],
            out_specs=pl.BlockSpec((dec_seq, WIDTH), lambda b, g, pt: (b, 0)),
            scratch_shapes=[
                pltpu.VMEM((page_rows, HEAD_DIM), F32),
                pltpu.VMEM((page_rows, HEAD_DIM), F32),
                pltpu.VMEM((rows, 1), F32),
                pltpu.VMEM((dec_seq, WIDTH), F32),
            ],
        ),
        out_shape=jax.ShapeDtypeStruct((dec_batch * dec_seq, WIDTH), F32),
        compiler_params=_cparams("parallel", "arbitrary"),
        name="sb_decode",
    )(page_table.reshape(-1), p, p, p, *([cache_k] * n_group), *([cache_v] * n_group), bias_rows, _tri(PAGE))


def _ret_consts(log_gamma, chunk):
    idx = jnp.arange(chunk, dtype=F32)
    diff = idx[:, None] - idx[None, :]
    causal = diff >= 0
    decay = jnp.where(causal[None], jnp.exp(jnp.where(causal, diff, 0.0)[None] * log_gamma[:, None, None]), 0.0)
    q_decay = jnp.exp((idx + 1.0)[None, :] * log_gamma[:, None])
    k_decay = jnp.exp((chunk - 1.0 - idx)[None, :] * log_gamma[:, None])
    lanes = (N_HEADS, chunk, HEAD_DIM)
    return (decay, jnp.broadcast_to(q_decay[:, :, None], lanes), jnp.broadcast_to(k_decay[:, :, None], lanes),
            jnp.exp(chunk * log_gamma))


def _ret_chunk(q, k, v, rg, state, decay, qd, kd, state_decay):
    qb, kb, vb = q.astype(BF16), k.astype(BF16), v.astype(BF16)
    scores = _dot_nt(qb, kb) * decay
    o = _dot(scores.astype(BF16), vb) + _dot(qb, state.astype(BF16)) * qd
    new_state = state_decay * state + lax.dot_general((k * kd).astype(BF16), vb, (((0,), (0,)), ((), ())),
                                                      preferred_element_type=F32)
    out = _rms_rows(o) * (rg * _sigmoid(rg))
    return out, new_state


def _ret_prompt_kernel(sd_ref, q_ref, k_ref, v_ref, rg_ref, decay_ref, qd_ref, kd_ref, o_ref, s_ref, *,
                       chunk, n_chunks, hg):
    h0 = pl.program_id(1) * hg
    s_ref[...] = jnp.zeros_like(s_ref)

    def body(c, _):
        rows = pl.ds(pl.multiple_of(c * chunk, chunk), chunk)
        for j in range(hg):
            sl = slice(j * HEAD_DIM, (j + 1) * HEAD_DIM)
            out, new_state = _ret_chunk(q_ref[rows, sl], k_ref[rows, sl], v_ref[rows, sl], rg_ref[rows, sl],
                                        s_ref[j], decay_ref[j], qd_ref[j], kd_ref[j], sd_ref[h0 + j])
            s_ref[j] = new_state
            o_ref[rows, sl] = out.astype(o_ref.dtype)
        return 0

    lax.fori_loop(0, n_chunks, body, 0)


def _ret_prompt(p, log_gamma, *, batch, seq, chunk, hg):
    decay, qd, kd, sd = _ret_consts(log_gamma, chunk)
    wide = hg * HEAD_DIM
    col = lambda sec: pl.BlockSpec((seq, wide), lambda b, h: (b, sec // hg + h))
    per_head = lambda n: pl.BlockSpec((hg, chunk, n), lambda b, h: (h, 0, 0))
    return pl.pallas_call(
        functools.partial(_ret_prompt_kernel, chunk=chunk, n_chunks=seq // chunk, hg=hg),
        grid=(batch, N_HEADS // hg),
        in_specs=[pl.BlockSpec(memory_space=pltpu.SMEM), col(SEC_RQ), col(SEC_RK), col(SEC_RV), col(SEC_RG),
                  per_head(chunk), per_head(HEAD_DIM), per_head(HEAD_DIM)],
        out_specs=[pl.BlockSpec((seq, wide), lambda b, h: (b, h)),
                   pl.BlockSpec((None, hg, HEAD_DIM, HEAD_DIM), lambda b, h: (b, h, 0, 0))],
        out_shape=[jax.ShapeDtypeStruct((batch * seq, WIDTH), BF16),
                   jax.ShapeDtypeStruct((batch, N_HEADS, HEAD_DIM, HEAD_DIM), F32)],
        compiler_params=_cparams("parallel", "parallel"),
        name="retention_prompt",
    )(sd, p, p, p, p, decay, qd, kd)


def _ret_sample_kernel(sd_ref, q_ref, k_ref, v_ref, rg_ref, s_in_ref, decay_ref, qd_ref, kd_ref, o_ref, s_out_ref):
    for h in range(N_HEADS):
        sl = slice(h * HEAD_DIM, (h + 1) * HEAD_DIM)
        out, new_state = _ret_chunk(q_ref[:, sl], k_ref[:, sl], v_ref[:, sl], rg_ref[:, sl], s_in_ref[h],
                                    decay_ref[h], qd_ref[h], kd_ref[h], sd_ref[h])
        o_ref[:, sl] = out
        s_out_ref[h] = new_state


def _ret_sample(p, state, log_gamma, *, n_prompt, dec_seq):
    dec_batch = state.shape[0]
    decay, qd, kd, sd = _ret_consts(log_gamma, dec_seq)
    row0 = n_prompt // dec_seq
    col = lambda sec: pl.BlockSpec((dec_seq, WIDTH), lambda b: (row0 + b, sec // N_HEADS))
    full = lambda a: pl.BlockSpec(a.shape, lambda b: (0,) * a.ndim)
    state_spec = pl.BlockSpec((None, N_HEADS, HEAD_DIM, HEAD_DIM), lambda b: (b, 0, 0, 0))
    return pl.pallas_call(
        _ret_sample_kernel,
        grid=(dec_batch,),
        in_specs=[pl.BlockSpec(memory_space=pltpu.SMEM), col(SEC_RQ), col(SEC_RK), col(SEC_RV), col(SEC_RG),
                  state_spec, full(decay), full(qd), full(kd)],
        out_specs=[pl.BlockSpec((dec_seq, WIDTH), lambda b: (b, 0)), state_spec],
        out_shape=[jax.ShapeDtypeStruct((dec_batch * dec_seq, WIDTH), F32),
                   jax.ShapeDtypeStruct(state.shape, F32)],
        compiler_params=_cparams("parallel"),
        name="retention_sample",
    )(sd, p, p, p, p, state, decay, qd, kd)


def _merge_kernel(ap_ref, as_ref, bp_ref, bs_ref, wa_ref, wb_ref, ga_ref, gb_ref, o_ref, *, prompt_tiles):
    def emit(a_ref, b_ref):
        ya = _dot(a_ref[...], wa_ref[...])
        yb = _dot(b_ref[...], wb_ref[...])
        o_ref[...] = (_sigmoid(ga_ref[...]) * ya + _sigmoid(gb_ref[...]) * yb).astype(o_ref.dtype)

    @pl.when(pl.program_id(0) < prompt_tiles)
    def _():
        emit(ap_ref, bp_ref)

    @pl.when(pl.program_id(0) >= prompt_tiles)
    def _():
        emit(as_ref, bs_ref)


def _merge(sb_p, sb_s, ret_p, ret_s, w_pa, w_pb, p, *, tm, tn):
    t = p.shape[0]
    d = w_pa.shape[1]
    ga0 = COL_GA // tn
    gb0 = (COL_GA + d) // tn
    assert sb_p.shape[0] % tm == 0 and sb_s.shape[0] % tm == 0
    prompt_tiles = sb_p.shape[0] // tm
    prompt_rows = pl.BlockSpec((tm, WIDTH), lambda i, j: (jnp.minimum(i, prompt_tiles - 1), 0))
    sample_rows = pl.BlockSpec((tm, WIDTH), lambda i, j: (jnp.maximum(i - prompt_tiles, 0), 0))
    return pl.pallas_call(
        functools.partial(_merge_kernel, prompt_tiles=prompt_tiles),
        grid=(t // tm, d // tn),
        in_specs=[
            prompt_rows, sample_rows, prompt_rows, sample_rows,
            pl.BlockSpec((WIDTH, tn), lambda i, j: (0, j)),
            pl.BlockSpec((WIDTH, tn), lambda i, j: (0, j)),
            pl.BlockSpec((tm, tn), lambda i, j: (i, ga0 + j)),
            pl.BlockSpec((tm, tn), lambda i, j: (i, gb0 + j)),
        ],
        out_specs=pl.BlockSpec((tm, tn), lambda i, j: (i, j)),
        out_shape=jax.ShapeDtypeStruct((t, d), BF16),
        compiler_params=_cparams("parallel", "arbitrary"),
        name="branch_merge",
    )(sb_p, sb_s, ret_p, ret_s, w_pa, w_pb, p, p)


def _out_proj_kernel(m_ref, w_ref, x_ref, o_ref):
    o_ref[...] = x_ref[...] + _dot(m_ref[...], w_ref[...])


def _out_proj(m, w_o, x, *, tm, tn):
    t, d = x.shape
    return pl.pallas_call(
        _out_proj_kernel,
        grid=(t // tm, d // tn),
        in_specs=[
            pl.BlockSpec((tm, d), lambda i, j: (i, 0)),
            pl.BlockSpec((d, tn), lambda i, j: (0, j)),
            pl.BlockSpec((tm, tn), lambda i, j: (i, j)),
        ],
        out_specs=pl.BlockSpec((tm, tn), lambda i, j: (i, j)),
        out_shape=jax.ShapeDtypeStruct((t, d), F32),
        compiler_params=_cparams("parallel", "arbitrary"),
        name="out_projection",
    )(m, w_o, x)


def _swiglu(h, wg, wu, wd):
    gate = _dot(h, wg)
    up = _dot(h, wu)
    return _dot((gate * _sigmoid(gate) * up).astype(BF16), wd)


def _dense_ffn_kernel(x_ref, g_ref, wg_ref, wu_ref, wd_ref, o_ref, h_ref):
    @pl.when(pl.program_id(1) == 0)
    def _():
        x = x_ref[...]
        h_ref[...] = (_rms_rows(x) * g_ref[...]).astype(BF16)
        o_ref[...] = x

    o_ref[...] += _swiglu(h_ref[...], wg_ref[...], wu_ref[...], wd_ref[...])


def _dense_ffn(x, g, wg, wu, wd, *, tm, tf):
    t, d = x.shape
    d_ff = wg.shape[1]
    return pl.pallas_call(
        _dense_ffn_kernel,
        grid=(t // tm, d_ff // tf),
        in_specs=[
            pl.BlockSpec((tm, d), lambda i, f: (i, 0), pipeline_mode=pl.Buffered(1)),
            pl.BlockSpec((1, d), lambda i, f: (0, 0)),
            pl.BlockSpec((d, tf), lambda i, f: (0, f)),
            pl.BlockSpec((d, tf), lambda i, f: (0, f)),
            pl.BlockSpec((tf, d), lambda i, f: (f, 0)),
        ],
        out_specs=pl.BlockSpec((tm, d), lambda i, f: (i, 0)),
        out_shape=jax.ShapeDtypeStruct((t, d), F32),
        scratch_shapes=[pltpu.VMEM((tm, d), BF16)],
        compiler_params=_cparams("parallel", "arbitrary"),
        name="dense_ffn",
    )(x, g.reshape(1, d), wg, wu, wd)


def _expert_ffn_kernel(te_ref, tr_ref, h_ref, wg_ref, wu_ref, wd_ref, o_ref, wgb_ref, wub_ref, wdb_ref, *, sub):
    i = pl.program_id(0)
    n_sub = (tr_ref[i] + (sub - 1)) // sub

    @pl.when(pl.program_id(1) == 0)
    def _():
        o_ref[...] = jnp.zeros_like(o_ref)

    @pl.when(n_sub > 0)
    def _():
        wgb_ref[...] = wg_ref[...].astype(BF16)
        wub_ref[...] = wu_ref[...].astype(BF16)
        wdb_ref[...] = wd_ref[...].astype(BF16)

        def run(first, count):
            rs = pl.ds(pl.multiple_of(first * sub, sub), count * sub)
            o_ref[rs, :] += _swiglu(h_ref[rs, :], wgb_ref[...], wub_ref[...], wdb_ref[...])

        def pair(s, _):
            run(2 * s, 2)
            return 0

        lax.fori_loop(0, n_sub // 2, pair, 0)

        @pl.when(n_sub % 2 == 1)
        def _():
            run(n_sub - 1, 1)


def _expert_ffn(hs, wg, wu, wd, tile_expert, tile_rows, *, tm, tf, sub):
    r, d = hs.shape
    d_ff = wg.shape[2]
    nf = d_ff // tf

    def f_idx(i, f, tr):
        return jnp.where(tr[i] > 0, f, nf - 1)

    return pl.pallas_call(
        functools.partial(_expert_ffn_kernel, sub=sub),
        grid_spec=pltpu.PrefetchScalarGridSpec(
            num_scalar_prefetch=2,
            grid=(r // tm, nf),
            in_specs=[
                pl.BlockSpec((tm, d), lambda i, f, te, tr: (i, 0), pipeline_mode=pl.Buffered(1)),
                pl.BlockSpec((None, d, tf), lambda i, f, te, tr: (te[i], 0, f_idx(i, f, tr))),
                pl.BlockSpec((None, d, tf), lambda i, f, te, tr: (te[i], 0, f_idx(i, f, tr))),
                pl.BlockSpec((None, tf, d), lambda i, f, te, tr: (te[i], f_idx(i, f, tr), 0)),
            ],
            out_specs=pl.BlockSpec((tm, d), lambda i, f, te, tr: (i, 0)),
            scratch_shapes=[pltpu.VMEM((d, tf), BF16), pltpu.VMEM((d, tf), BF16), pltpu.VMEM((tf, d), BF16)],
        ),
        out_shape=jax.ShapeDtypeStruct((r, d), F32),
        compiler_params=_cparams("parallel", "arbitrary"),
        name="expert_ffn",
    )(tile_expert, tile_rows, hs, wg, wu, wd)


def _router_kernel(x_ref, g_ref, w_ref, o_ref, hn_ref, *, n_experts):
    h = _rms_rows(x_ref[...]) * g_ref[...]
    hn_ref[...] = h
    w = w_ref[...]
    h_hi = h.astype(BF16)
    h_lo = (h - h_hi.astype(F32)).astype(BF16)
    w_hi = w.astype(BF16)
    w_lo = (w - w_hi.astype(F32)).astype(BF16)
    logits = _dot(h_hi, w_hi) + (_dot(h_hi, w_lo) + _dot(h_lo, w_hi))
    lane = lax.broadcasted_iota(jnp.int32, logits.shape, 1).astype(F32)
    lg = jnp.where(lane < n_experts, logits, NEG_INF)
    m1 = jnp.max(lg, axis=-1, keepdims=True)
    i1 = jnp.min(jnp.where(lg == m1, lane, float(HEAD_DIM)), axis=-1, keepdims=True)
    lg2 = jnp.where(lane == i1, NEG_INF, lg)
    m2 = jnp.max(lg2, axis=-1, keepdims=True)
    i2 = jnp.min(jnp.where(lg2 == m2, lane, float(HEAD_DIM)), axis=-1, keepdims=True)
    e = jnp.exp(m2 - m1)
    g1 = 1.0 / (1.0 + e)
    g2 = e / (1.0 + e)
    o_ref[...] = jnp.where(lane == 0, i1, jnp.where(lane == 1, i2, jnp.where(lane == 2, g1, jnp.where(lane == 3, g2, 0.0))))


def _router(x, g, w_router, *, tm):
    t, d = x.shape
    n_experts = w_router.shape[1]
    w_pad = jnp.pad(w_router, ((0, 0), (0, HEAD_DIM - n_experts)))
    return pl.pallas_call(
        functools.partial(_router_kernel, n_experts=n_experts),
        grid=(t // tm,),
        in_specs=[
            pl.BlockSpec((tm, d), lambda i: (i, 0)),
            pl.BlockSpec((1, d), lambda i: (0, 0)),
            pl.BlockSpec((d, HEAD_DIM), lambda i: (0, 0)),
        ],
        out_specs=[pl.BlockSpec((tm, HEAD_DIM), lambda i: (i, 0)), pl.BlockSpec((tm, d), lambda i: (i, 0))],
        out_shape=[jax.ShapeDtypeStruct((t, HEAD_DIM), F32), jax.ShapeDtypeStruct((t, d), F32)],
        compiler_params=_cparams("parallel"),
        name="router",
    )(x, g.reshape(1, d), w_pad)


def _row_copy(src_hbm, dst_ref, sem, src_row, dst_row):
    return pltpu.make_async_copy(src_hbm.at[pl.ds(src_row, 1)], dst_ref.at[pl.ds(dst_row, 1)], sem)


def _gather_rows_kernel(idx_ref, idx_next_ref, hn_hbm, o_ref, buf_ref, sem, *, rows):
    i = pl.program_id(0)
    slot = i % 2

    def start_all(ids_ref, s):
        def start(r, _):
            _row_copy(hn_hbm, buf_ref.at[s], sem.at[s], ids_ref[0, r], r).start()
            return 0
        lax.fori_loop(0, rows, start, 0, unroll=8)

    @pl.when(i == 0)
    def _():
        start_all(idx_ref, slot)

    @pl.when(i + 1 < pl.num_programs(0))
    def _():
        start_all(idx_next_ref, 1 - slot)

    def wait(r, _):
        _row_copy(hn_hbm, buf_ref.at[slot], sem.at[slot], 0, r).wait()
        return 0

    lax.fori_loop(0, rows, wait, 0, unroll=8)
    o_ref[...] = buf_ref[slot].astype(BF16)


def _gather_rows(hn, row_token, *, rows):
    r = row_token.shape[0]
    d = hn.shape[1]
    steps = r // rows
    ids = row_token.reshape(steps, 1, rows)
    return pl.pallas_call(
        functools.partial(_gather_rows_kernel, rows=rows),
        grid=(steps,),
        in_specs=[
            pl.BlockSpec((None, 1, rows), lambda i: (i, 0, 0), memory_space=pltpu.SMEM),
            pl.BlockSpec((None, 1, rows), lambda i: (jnp.minimum(i + 1, steps - 1), 0, 0), memory_space=pltpu.SMEM),
            pl.BlockSpec(memory_space=pl.ANY),
        ],
        out_specs=pl.BlockSpec((rows, d), lambda i: (i, 0)),
        out_shape=jax.ShapeDtypeStruct((r, d), BF16),
        scratch_shapes=[pltpu.VMEM((2, rows, d), F32), pltpu.SemaphoreType.DMA((2,))],
        compiler_params=_cparams("arbitrary"),
        name="gather_expert_rows",
    )(ids, ids, hn)


def _combine_kernel(pos_ref, pos_next_ref, x_ref, route_ref, y_hbm, op_ref, os_ref, ya_ref, yb_ref, sem, *,
                    rows, prompt_steps):
    i = pl.program_id(0)
    slot = i % 2

    def start_all(p_ref, s):
        def start(r, _):
            _row_copy(y_hbm, ya_ref.at[s], sem.at[s], p_ref[0, r], r).start()
            _row_copy(y_hbm, yb_ref.at[s], sem.at[s], p_ref[0, rows + r], r).start()
            return 0
        lax.fori_loop(0, rows, start, 0, unroll=8)

    @pl.when(i == 0)
    def _():
        start_all(pos_ref, slot)

    @pl.when(i + 1 < pl.num_programs(0))
    def _():
        start_all(pos_next_ref, 1 - slot)

    def wait(r, _):
        _row_copy(y_hbm, ya_ref.at[slot], sem.at[slot], 0, r).wait()
        _row_copy(y_hbm, yb_ref.at[slot], sem.at[slot], 0, r).wait()
        return 0

    lax.fori_loop(0, rows, wait, 0, unroll=8)
    route = route_ref[...]
    out = x_ref[...] + route[:, 2:3] * ya_ref[slot] + route[:, 3:4] * yb_ref[slot]

    @pl.when(i < prompt_steps)
    def _():
        op_ref[...] = out

    @pl.when(i >= prompt_steps)
    def _():
        os_ref[...] = out


def _combine(x, route, y_sorted, pos, *, rows, n_prompt):
    t, d = x.shape
    assert n_prompt % rows == 0 and 0 < n_prompt < t
    prompt_steps = n_prompt // rows
    steps = t // rows
    return pl.pallas_call(
        functools.partial(_combine_kernel, rows=rows, prompt_steps=prompt_steps),
        grid=(steps,),
        in_specs=[
            pl.BlockSpec((None, 1, TOP_K * rows), lambda i: (i, 0, 0), memory_space=pltpu.SMEM),
            pl.BlockSpec((None, 1, TOP_K * rows), lambda i: (jnp.minimum(i + 1, steps - 1), 0, 0),
                         memory_space=pltpu.SMEM),
            pl.BlockSpec((rows, d), lambda i: (i, 0)),
            pl.BlockSpec((rows, HEAD_DIM), lambda i: (i, 0)),
            pl.BlockSpec(memory_space=pl.ANY),
        ],
        out_specs=[pl.BlockSpec((rows, d), lambda i: (jnp.minimum(i, prompt_steps - 1), 0)),
                   pl.BlockSpec((rows, d), lambda i: (jnp.maximum(i - prompt_steps, 0), 0))],
        out_shape=[jax.ShapeDtypeStruct((n_prompt, d), F32), jax.ShapeDtypeStruct((t - n_prompt, d), F32)],
        scratch_shapes=[pltpu.VMEM((2, rows, d), F32), pltpu.VMEM((2, rows, d), F32), pltpu.SemaphoreType.DMA((2,))],
        compiler_params=_cparams("arbitrary"),
        name="combine_expert_rows",
    )(pos, pos, x, route, y_sorted)


def _routing_tables(route, n_experts, *, tm, rows):
    t = route.shape[0]
    expert = route[:, :TOP_K].astype(jnp.int32).reshape(-1)
    onehot = (expert[:, None] == jnp.arange(n_experts)[None, :]).astype(jnp.int32)
    rank = jnp.sum((jnp.cumsum(onehot, axis=0) - onehot) * onehot, axis=1)
    count = jnp.sum(onehot, axis=0)
    tiles = (count + tm - 1) // tm
    tile_end = jnp.cumsum(tiles)
    start = (tile_end - tiles) * tm
    dest = start[expert] + rank
    n_rows = (t * TOP_K // tm + n_experts) * tm
    n_tiles = n_rows // tm
    row_token = jnp.zeros((n_rows,), jnp.int32).at[dest].set(jnp.arange(t * TOP_K, dtype=jnp.int32) // TOP_K)
    tile_id = jnp.arange(n_tiles)
    tile_valid = tile_id < tile_end[-1]
    tile_expert = jnp.minimum(jnp.sum(tile_id[:, None] >= tile_end[None, :], axis=1), n_experts - 1)
    last_expert = tile_expert[jnp.maximum(tile_end[-1] - 1, 0)]
    first_tile = (tile_end - tiles)[tile_expert]
    tile_rows = jnp.clip(count[tile_expert] - (tile_id - first_tile) * tm, 0, tm)
    tile_rows = jnp.where(tile_valid, tile_rows, 0).astype(jnp.int32)
    tile_expert = jnp.where(tile_valid, tile_expert, last_expert).astype(jnp.int32)
    dest = dest.reshape(t // rows, rows, TOP_K)
    pos = jnp.concatenate([dest[:, :, 0], dest[:, :, 1]], axis=1).reshape(t // rows, 1, TOP_K * rows)
    return row_token, tile_expert, tile_rows, pos


def _moe(x, g, w_router, wg, wu, wd, *, n_prompt, tm_route, tm, tf, sub, rows):
    n_experts = w_router.shape[1]
    route, hn = _router(x, g, w_router, tm=tm_route)
    row_token, tile_expert, tile_rows, pos = _routing_tables(route, n_experts, tm=tm, rows=rows)
    hs = _gather_rows(hn, row_token, rows=_pick(tm, 4 * rows))
    ys = _expert_ffn(hs, wg, wu, wd, tile_expert, tile_rows, tm=tm, tf=tf, sub=sub)
    return _combine(x, route, ys, pos, rows=rows, n_prompt=n_prompt)


def _kv_rows_kernel(*refs, depth, tr, prompt_steps):
    ins, (kp_ref, vp_ref, ks_ref, vs_ref) = refs[:2 * depth], refs[2 * depth:]
    l = pl.program_id(0)
    i = pl.program_id(1)

    def emit(src_ref, dst_ref):
        for h in range(N_HEADS):
            dst_ref[pl.ds(h, tr, stride=N_HEADS), :] = src_ref[:, h * HEAD_DIM:(h + 1) * HEAD_DIM]

    for layer in range(depth):
        @pl.when(jnp.logical_and(l == layer, i < prompt_steps))
        def _():
            emit(ins[2 * layer], kp_ref)
            emit(ins[2 * layer + 1], vp_ref)

        @pl.when(jnp.logical_and(l == layer, i >= prompt_steps))
        def _():
            emit(ins[2 * layer], ks_ref)
            emit(ins[2 * layer + 1], vs_ref)


def _kv_rows(projections, *, n_prompt, tr):
    depth = len(projections)
    t = projections[0].shape[0]
    assert n_prompt % tr == 0 and (t - n_prompt) % tr == 0
    steps = t // tr
    prompt_steps = n_prompt // tr

    def src(layer, sec):
        def index(l, i):
            return (jnp.where(l == layer, i, jnp.where(l < layer, 0, steps - 1)), sec // N_HEADS)
        return pl.BlockSpec((tr, WIDTH), index)

    prompt_spec = pl.BlockSpec((None, tr * N_HEADS, HEAD_DIM), lambda l, i: (l, jnp.minimum(i, prompt_steps - 1), 0))
    sample_spec = pl.BlockSpec((None, tr * N_HEADS, HEAD_DIM), lambda l, i: (l, jnp.maximum(i - prompt_steps, 0), 0))
    prompt_shape = jax.ShapeDtypeStruct((depth, n_prompt * N_HEADS, HEAD_DIM), F32)
    sample_shape = jax.ShapeDtypeStruct((depth, (t - n_prompt) * N_HEADS, HEAD_DIM), F32)
    return pl.pallas_call(
        functools.partial(_kv_rows_kernel, depth=depth, tr=tr, prompt_steps=prompt_steps),
        grid=(depth, steps),
        in_specs=[src(layer, sec) for layer in range(depth) for sec in (SEC_SK, SEC_SV)],
        out_specs=[prompt_spec, prompt_spec, sample_spec, sample_spec],
        out_shape=[prompt_shape, prompt_shape, sample_shape, sample_shape],
        compiler_params=_cparams("arbitrary", "arbitrary"),
        name="kv_rows",
    )(*[p for p in projections for _ in range(2)])


def _pick(n, want):
    t = min(n, want)
    while n % t:
        t -= 8
    return t


def kernel(x_prompt, x_sample, cache_sb_k, cache_sb_v, state_ret, page_table, norm_attn, w_in, qnorm_g, knorm_g,
           sb_bias, w_pa, w_pb, w_o, norm_ffn, w_ff_gate, w_ff_up, w_ff_down, w_router, w_exp_gate, w_exp_up,
           w_exp_down):
    batch, seq, d = x_prompt.shape
    dec_batch, dec_seq, _ = x_sample.shape
    depth = w_in.shape[0]
    n_pages = page_table.shape[1]
    past_len = n_pages * cache_sb_k.shape[2]
    n_prompt = batch * seq
    n_sample = dec_batch * dec_seq
    t = n_prompt + n_sample
    log_gamma = jnp.log1p(-jnp.exp2(-5.0 - jnp.arange(N_HEADS, dtype=F32)))

    tm = _pick(int(np.gcd(seq, n_sample)), 1024)
    tq = _pick(seq, 256)
    chunk = _pick(seq, 128)
    tm_moe = _pick(TOP_K * t, 1024)
    rope_tab = _rope_table(seq, dec_seq, past_len, tm)
    w_in, w_pa, w_pb, w_o, w_ff_gate, w_ff_up, w_ff_down = (
        w.astype(BF16) for w in (w_in, w_pa, w_pb, w_o, w_ff_gate, w_ff_up, w_ff_down))

    x = jnp.concatenate([x_prompt.reshape(n_prompt, d), x_sample.reshape(n_sample, d)], axis=0)
    projections, sp, ss = [], [], []
    y_prompt = y_sample = None
    for l in range(depth):
        p = _in_projection(x, norm_attn[l], w_in[l], rope_tab, qnorm_g[l], knorm_g[l],
                           n_prompt=n_prompt, seq=seq, tm=tm, tn=1024)
        projections.append(p)

        o_sb_p = _sb_prompt(p, sb_bias[l], batch=batch, seq=seq, tq=tq, hg=4)
        o_sb_s = _sb_decode(p, cache_sb_k, cache_sb_v, page_table, sb_bias[l], layer=l, n_prompt=n_prompt,
                            dec_seq=dec_seq, n_group=_pick(n_pages * 8, 64) // 8)
        ret_p, state_p = _ret_prompt(p, log_gamma, batch=batch, seq=seq, chunk=chunk, hg=4)
        ret_s, state_s = _ret_sample(p, state_ret[l], log_gamma, n_prompt=n_prompt, dec_seq=dec_seq)
        sp.append(state_p)
        ss.append(state_s)

        m = _merge(o_sb_p, o_sb_s.astype(BF16), ret_p, ret_s.astype(BF16), w_pa[l], w_pb[l], p, tm=tm, tn=1024)
        x = _out_proj(m, w_o[l], x, tm=tm, tn=1024)

        i = l // 2
        if l % 2 == 0:
            x = _dense_ffn(x, norm_ffn[l], w_ff_gate[i], w_ff_up[i], w_ff_down[i], tm=tm, tf=512)
        else:
            y_prompt, y_sample = _moe(x, norm_ffn[l], w_router[i], w_exp_gate[i], w_exp_up[i], w_exp_down[i],
                                      n_prompt=n_prompt, tm_route=_pick(t, 512), tm=tm_moe, tf=512,
                                      sub=_pick(tm_moe, 256), rows=_pick(int(np.gcd(tm_moe, n_sample)), 128))
            if l + 1 < depth:
                x = jnp.concatenate([y_prompt, y_sample], axis=0)
    if depth % 2:
        y_prompt, y_sample = x[:n_prompt], x[n_prompt:]

    kp, vp, ks, vs = _kv_rows(projections, n_prompt=n_prompt, tr=_pick(int(np.gcd(n_prompt, n_sample)), 256))
    rows_p = (depth, batch, seq, N_HEADS, HEAD_DIM)
    rows_s = (depth, dec_batch, dec_seq, N_HEADS, HEAD_DIM)
    return (y_prompt.reshape(batch, seq, d), y_sample.reshape(dec_batch, dec_seq, d),
            kp.reshape(rows_p), vp.reshape(rows_p), jnp.stack(sp), ks.reshape(rows_s), vs.reshape(rows_s), jnp.stack(ss))
```
